```python
import jax, jax.numpy as jnp
from jax import lax
import numpy as np

D_MODEL = 2048
BATCH = 4
SEQ = 4096
DEPTH = 2

N_MIXERS = 2
N_GMLP_LAYERS = (DEPTH + 1) // 2
N_SWA_LAYERS = DEPTH // 2
D_FF = 5632
FFN_RESIDUAL_WEIGHT = 0.5
PLE_DIM = 256
RMS_EPS = 1e-6
LN_EPS = 1e-5
CHUNK = 128
GMLP_WIDTH = 2 * D_MODEL
GMLP_GROUPS = 16
GMLP_GROUP_DIM = GMLP_WIDTH // GMLP_GROUPS
N_Q_HEADS = 32
N_KV_HEADS = 4
HEAD_DIM = 64
Q_PER_KV = N_Q_HEADS // N_KV_HEADS
WINDOW = 128
ROPE_THETA = 500000.0
ROPE_DIM = HEAD_DIM // 4

kernel_name = "hybrid_gmlp_swa_sink_macaron"


def rms_norm(x, g):
    xf = x.astype(jnp.float32)
    y = xf * lax.rsqrt(jnp.mean(xf * xf, axis=-1, keepdims=True) + RMS_EPS)
    return (y * g.astype(jnp.float32)).astype(x.dtype)


def layer_norm(x, g, b):
    xf = x.astype(jnp.float32)
    mu = jnp.mean(xf, axis=-1, keepdims=True)
    var = jnp.mean(jnp.square(xf - mu), axis=-1, keepdims=True)
    y = (xf - mu) * lax.rsqrt(var + LN_EPS)
    return (y * g.astype(jnp.float32) + b.astype(jnp.float32)).astype(x.dtype)


def swiglu(h, w1, w3, w2):
    return (jax.nn.silu(h @ w1) * (h @ w3)) @ w2


def gmlp_chunk_mixer(h, w_in, ln_g, ln_b, w_s, b_s, w_out):
    B, S, _ = h.shape
    z = jax.nn.gelu(h @ w_in, approximate=False)
    u, v = jnp.split(z, 2, axis=-1)
    v = layer_norm(v, ln_g, ln_b)
    v = v.reshape(B, S // CHUNK, CHUNK, GMLP_GROUPS, GMLP_GROUP_DIM)
    causal = jnp.tril(jnp.ones((CHUNK, CHUNK), dtype=bool))
    w = jnp.where(causal[None], w_s, jnp.zeros((), w_s.dtype))
    s = jnp.einsum('gts,bcsgd->bctgd', w, v) + b_s.T[:, :, None]
    gated = u * s.reshape(B, S, GMLP_WIDTH)
    return gated @ w_out


def rope_tables(S):
    inv_freq = ROPE_THETA ** (-jnp.arange(0, ROPE_DIM, 2, dtype=jnp.float32) / ROPE_DIM)
    ang = jnp.arange(S, dtype=jnp.float32)[:, None] * inv_freq[None, :]
    return jnp.cos(ang), jnp.sin(ang)


def apply_partial_rope(x, cos, sin):
    half = ROPE_DIM // 2
    c = cos[:, None, :].astype(x.dtype)
    s = sin[:, None, :].astype(x.dtype)
    x1 = x[..., :half]
    x2 = x[..., half:ROPE_DIM]
    return jnp.concatenate([x1 * c - x2 * s, x2 * c + x1 * s, x[..., ROPE_DIM:]], axis=-1)


def swa_sink_attention(h, wq, bq, wk, bk, wv, bv, sinks, wo, bo):
    B, S, _ = h.shape
    NB = S // WINDOW
    q = (h @ wq + bq).reshape(B, S, N_Q_HEADS, HEAD_DIM)
    k = (h @ wk + bk).reshape(B, S, N_KV_HEADS, HEAD_DIM)
    v = (h @ wv + bv).reshape(B, S, N_KV_HEADS, HEAD_DIM)
    cos, sin = rope_tables(S)
    q = apply_partial_rope(q, cos, sin)
    k = apply_partial_rope(k, cos, sin)
    q = q.reshape(B, NB, WINDOW, N_KV_HEADS, Q_PER_KV, HEAD_DIM)
    k = k.reshape(B, NB, WINDOW, N_KV_HEADS, HEAD_DIM)
    v = v.reshape(B, NB, WINDOW, N_KV_HEADS, HEAD_DIM)
    pad = ((0, 0), (1, 0), (0, 0), (0, 0), (0, 0))
    kb = jnp.concatenate([jnp.pad(k, pad)[:, :-1], k], axis=2)
    vb = jnp.concatenate([jnp.pad(v, pad)[:, :-1], v], axis=2)
    scores = jnp.einsum('bnqkgd,bnskd->bnkgqs', q, kb).astype(jnp.float32) * (HEAD_DIM ** -0.5)
    qpos = jnp.arange(WINDOW)[:, None] + WINDOW
    kpos = jnp.arange(2 * WINDOW)[None, :]
    diff = qpos - kpos
    band = (diff >= 0) & (diff < WINDOW)
    has_prev = (jnp.arange(NB) > 0)[:, None, None]
    valid = jnp.where(has_prev, band[None], (band & (kpos >= WINDOW))[None])
    scores = jnp.where(valid[None, :, None, None], scores, -jnp.inf)
    sink = sinks.astype(jnp.float32).reshape(N_KV_HEADS, Q_PER_KV)[None, None, :, :, None, None]
    m = jnp.maximum(jnp.max(scores, axis=-1, keepdims=True), sink)
    e = jnp.exp(scores - m)
    denom = jnp.sum(e, axis=-1, keepdims=True) + jnp.exp(sink - m)
    probs = (e / denom).astype(vb.dtype)
    o = jnp.einsum('bnkgqs,bnskd->bnqkgd', probs, vb).reshape(B, S, N_Q_HEADS * HEAD_DIM)
    return o @ wo + bo


def per_layer_embedding(x, p_i, g, w_gate, w_proj):
    gate = jax.nn.sigmoid(rms_norm(x, g) @ w_gate)
    return gate * (p_i @ w_proj)


def setup_inputs(seed: int = 0) -> dict:
    key = jax.random.key(seed)
    ks = iter(jax.random.split(key, 40))
    f32 = jnp.float32

    def dense(shape, scale=1.0):
        return jax.random.normal(next(ks), shape, f32) * (scale * shape[-2] ** -0.5)

    def gain(shape):
        return 1.0 + 0.02 * jax.random.normal(next(ks), shape, f32)

    def bias(shape, s=0.02):
        return s * jax.random.normal(next(ks), shape, f32)

    out_scale = (2.0 * DEPTH) ** -0.5
    NA, NBL = N_GMLP_LAYERS, N_SWA_LAYERS
    QW = N_Q_HEADS * HEAD_DIM
    KW = N_KV_HEADS * HEAD_DIM
    return {
        "x": jax.random.normal(next(ks), (BATCH, SEQ, D_MODEL), f32),
        "p": jax.random.normal(next(ks), (DEPTH, BATCH, SEQ, PLE_DIM), f32),
        "ffn1_norm": gain((DEPTH, D_MODEL)),
        "ffn1_w1": dense((DEPTH, D_MODEL, D_FF)),
        "ffn1_w3": dense((DEPTH, D_MODEL, D_FF)),
        "ffn1_w2": dense((DEPTH, D_FF, D_MODEL), out_scale),
        "mix_norm": gain((DEPTH, D_MODEL)),
        "ffn2_norm": gain((DEPTH, D_MODEL)),
        "ffn2_w1": dense((DEPTH, D_MODEL, D_FF)),
        "ffn2_w3": dense((DEPTH, D_MODEL, D_FF)),
        "ffn2_w2": dense((DEPTH, D_FF, D_MODEL), out_scale),
        "ple_norm": gain((DEPTH, D_MODEL)),
        "ple_w_gate": dense((DEPTH, D_MODEL, D_MODEL)),
        "ple_w_proj": dense((DEPTH, PLE_DIM, D_MODEL), out_scale),
        "gmlp_w_in": dense((NA, D_MODEL, 2 * GMLP_WIDTH)),
        "gmlp_ln_g": gain((NA, GMLP_WIDTH)),
        "gmlp_ln_b": bias((NA, GMLP_WIDTH)),
        "gmlp_w_s": dense((NA, GMLP_GROUPS, CHUNK, CHUNK), 0.5),
        "gmlp_b_s": 1.0 + bias((NA, GMLP_GROUPS, CHUNK)),
        "gmlp_w_out": dense((NA, GMLP_WIDTH, D_MODEL), out_scale),
        "swa_wq": dense((NBL, D_MODEL, QW)),
        "swa_bq": bias((NBL, QW)),
        "swa_wk": dense((NBL, D_MODEL, KW)),
        "swa_bk": bias((NBL, KW)),
        "swa_wv": dense((NBL, D_MODEL, KW)),
        "swa_bv": bias((NBL, KW)),
        "swa_sinks": 0.5 * jax.random.normal(next(ks), (NBL, N_Q_HEADS), f32),
        "swa_wo": dense((NBL, QW, D_MODEL), out_scale),
        "swa_bo": bias((NBL, D_MODEL)),
        "final_norm": gain((D_MODEL,)),
    }


def reference(x, p, ffn1_norm, ffn1_w1, ffn1_w3, ffn1_w2, mix_norm,
              ffn2_norm, ffn2_w1, ffn2_w3, ffn2_w2,
              ple_norm, ple_w_gate, ple_w_proj,
              gmlp_w_in, gmlp_ln_g, gmlp_ln_b, gmlp_w_s, gmlp_b_s, gmlp_w_out,
              swa_wq, swa_bq, swa_wk, swa_bk, swa_wv, swa_bv, swa_sinks, swa_wo, swa_bo,
              final_norm):
    for i in range(DEPTH):
        x = x + FFN_RESIDUAL_WEIGHT * swiglu(rms_norm(x, ffn1_norm[i]), ffn1_w1[i], ffn1_w3[i], ffn1_w2[i])
        h = rms_norm(x, mix_norm[i])
        j = i // N_MIXERS
        if i % N_MIXERS == 0:
            x = x + gmlp_chunk_mixer(h, gmlp_w_in[j], gmlp_ln_g[j], gmlp_ln_b[j],
                                     gmlp_w_s[j], gmlp_b_s[j], gmlp_w_out[j])
        else:
            x = x + swa_sink_attention(h, swa_wq[j], swa_bq[j], swa_wk[j], swa_bk[j],
                                       swa_wv[j], swa_bv[j], swa_sinks[j], swa_wo[j], swa_bo[j])
        x = x + FFN_RESIDUAL_WEIGHT * swiglu(rms_norm(x, ffn2_norm[i]), ffn2_w1[i], ffn2_w3[i], ffn2_w2[i])
        x = x + per_layer_embedding(x, p[i], ple_norm[i], ple_w_gate[i], ple_w_proj[i])
    return rms_norm(x, final_norm)
```

```python
import functools
import math

import jax
import jax.numpy as jnp
from jax import lax
from jax.experimental import pallas as pl
from jax.experimental.pallas import tpu as pltpu

F32 = jnp.float32
BF16 = jnp.bfloat16

D_MODEL = 2048
DEPTH = 2
D_FF = 5632
PLE_DIM = 256
RMS_EPS = 1e-6
LN_EPS = 1e-5
CHUNK = 128
GMLP_WIDTH = 2 * D_MODEL
GMLP_GROUPS = 16
GMLP_GROUP_DIM = GMLP_WIDTH // GMLP_GROUPS
N_Q_HEADS = 32
N_KV_HEADS = 4
HEAD_DIM = 64
Q_PER_KV = N_Q_HEADS // N_KV_HEADS
WINDOW = 128
ROPE_THETA = 500000.0
ROPE_DIM = HEAD_DIM // 4
QW = N_Q_HEADS * HEAD_DIM
KW = N_KV_HEADS * HEAD_DIM

LANES = 128

TM = 512
TM_PLE = 256
TF = 512
TN = 512
N_IN_STEPS = 2 * GMLP_WIDTH // TN
N_HALF = GMLP_WIDTH // TN
GROUPS_PER_SLAB = TN // GMLP_GROUP_DIM
VMEM_LIMIT = 56 * 1024 * 1024


def _cparams(sem):
    return pltpu.CompilerParams(dimension_semantics=sem, vmem_limit_bytes=VMEM_LIMIT)


def _dot(a, b):
    return jnp.dot(a, b, preferred_element_type=F32)


def _rms(xf, g):
    ms = jnp.mean(xf * xf, axis=-1, keepdims=True)
    return xf * lax.rsqrt(ms + RMS_EPS) * g


def _ffn_kernel(x_ref, g_ref, w1_ref, w3_ref, w2_ref, o_ref, h_ref):
    f = pl.program_id(1)

    @pl.when(f == 0)
    def _():
        xf = x_ref[...]
        h_ref[...] = _rms(xf, g_ref[...]).astype(BF16)
        o_ref[...] = xf

    h = h_ref[...]
    a = _dot(h, w1_ref[...])
    b = _dot(h, w3_ref[...])
    gate = (0.5 * (a * jax.nn.sigmoid(a)) * b).astype(BF16)
    o_ref[...] += _dot(gate, w2_ref[...])


def _ffn(x, g, w1, w3, w2):
    t = x.shape[0]
    return pl.pallas_call(
        _ffn_kernel,
        grid=(t // TM, D_FF // TF),
        in_specs=[
            pl.BlockSpec((TM, D_MODEL), lambda i, f: (i, 0)),
            pl.BlockSpec((1, D_MODEL), lambda i, f: (0, 0)),
            pl.BlockSpec((D_MODEL, TF), lambda i, f: (0, f)),
            pl.BlockSpec((D_MODEL, TF), lambda i, f: (0, f)),
            pl.BlockSpec((TF, D_MODEL), lambda i, f: (f, 0)),
        ],
        out_specs=pl.BlockSpec((TM, D_MODEL), lambda i, f: (i, 0)),
        out_shape=jax.ShapeDtypeStruct((t, D_MODEL), F32),
        scratch_shapes=[pltpu.VMEM((TM, D_MODEL), BF16)],
        compiler_params=_cparams(("parallel", "arbitrary")),
        name="ffn",
    )(x, g, w1, w3, w2)


def _gmlp_kernel(x_ref, g_ref, win_ref, lng_ref, lnb_ref, ws_ref, bs_ref, wout_ref,
                 o_ref, h_ref, u_ref, v_ref, gt_ref):
    j = pl.program_id(1)

    @pl.when(j == 0)
    def _():
        h_ref[...] = _rms(x_ref[...], g_ref[...]).astype(BF16)

    @pl.when(j < N_IN_STEPS)
    def _():
        z = _dot(h_ref[...], win_ref[...])
        z = 0.5 * z * (1.0 + lax.erf(z * (1.0 / math.sqrt(2.0))))

        @pl.when(j < N_HALF)
        def _():
            u_ref[j] = z.astype(BF16)

        @pl.when(j >= N_HALF)
        def _():
            v_ref[j - N_HALF] = z

    @pl.when(j == N_IN_STEPS - 1)
    def _():
        total = jnp.zeros((TM, 1), F32)
        for s in range(N_HALF):
            total += jnp.sum(v_ref[s], axis=-1, keepdims=True)
        mu = total * (1.0 / GMLP_WIDTH)
        sq = jnp.zeros((TM, 1), F32)
        for s in range(N_HALF):
            d = v_ref[s] - mu
            sq += jnp.sum(d * d, axis=-1, keepdims=True)
        rstd = lax.rsqrt(sq * (1.0 / GMLP_WIDTH) + LN_EPS)
        for s in range(N_HALF):
            v_ref[s] = (v_ref[s] - mu) * rstd * lng_ref[s] + lnb_ref[s]

    @pl.when(j >= N_IN_STEPS)
    def _():
        k = j - N_IN_STEPS
        vn = v_ref[k].astype(BF16)
        u = u_ref[k]
        row = lax.broadcasted_iota(jnp.int32, (CHUNK, CHUNK), 0)
        col = lax.broadcasted_iota(jnp.int32, (CHUNK, CHUNK), 1)
        causal = row >= col
        for gg in range(GROUPS_PER_SLAB):
            grp = GROUPS_PER_SLAB * k + gg
            w = jnp.where(causal, ws_ref[grp], 0.0).astype(BF16)
            bias = bs_ref[grp]
            cols = slice(gg * GMLP_GROUP_DIM, (gg + 1) * GMLP_GROUP_DIM)
            for c in range(TM // CHUNK):
                rows = slice(c * CHUNK, (c + 1) * CHUNK)
                s = _dot(w, vn[rows, cols]) + bias
                gt_ref[rows, cols] = (u[rows, cols].astype(F32) * s).astype(BF16)

        @pl.when(k == 0)
        def _():
            o_ref[...] = x_ref[...]

        o_ref[...] += _dot(gt_ref[...], wout_ref[...])


def _gmlp(x, g, w_in, ln_g, ln_b, w_s, b_s, w_out):
    t = x.shape[0]
    n_steps = N_IN_STEPS + N_HALF
    return pl.pallas_call(
        _gmlp_kernel,
        grid=(t // TM, n_steps),
        in_specs=[
            pl.BlockSpec((TM, D_MODEL), lambda i, j: (i, 0)),
            pl.BlockSpec((1, D_MODEL), lambda i, j: (0, 0)),
            pl.BlockSpec((D_MODEL, TN), lambda i, j: (0, jnp.minimum(j, N_IN_STEPS - 1))),
            pl.BlockSpec((N_HALF, 1, TN), lambda i, j: (0, 0, 0)),
            pl.BlockSpec((N_HALF, 1, TN), lambda i, j: (0, 0, 0)),
            pl.BlockSpec((GMLP_GROUPS, CHUNK, CHUNK), lambda i, j: (0, 0, 0)),
            pl.BlockSpec((GMLP_GROUPS, CHUNK, GMLP_GROUP_DIM), lambda i, j: (0, 0, 0)),
            pl.BlockSpec((TN, D_MODEL), lambda i, j: (jnp.maximum(j - N_IN_STEPS, 0), 0)),
        ],
        out_specs=pl.BlockSpec((TM, D_MODEL), lambda i, j: (i, 0)),
        out_shape=jax.ShapeDtypeStruct((t, D_MODEL), F32),
        scratch_shapes=[
            pltpu.VMEM((TM, D_MODEL), BF16),
            pltpu.VMEM((N_HALF, TM, TN), BF16),
            pltpu.VMEM((N_HALF, TM, TN), F32),
            pltpu.VMEM((TM, TN), BF16),
        ],
        compiler_params=_cparams(("parallel", "arbitrary")),
        name="gmlp",
    )(x, g, w_in, ln_g, ln_b, w_s, b_s, w_out)


def _rope(y, ca, cp, cm):
    n = y.shape[-1]
    reps = n // LANES
    ca = jnp.concatenate([ca] * reps, axis=-1)
    cp = jnp.concatenate([cp] * reps, axis=-1)
    cm = jnp.concatenate([cm] * reps, axis=-1)
    half = ROPE_DIM // 2
    return y * ca + pltpu.roll(y, n - half, 1) * cp + pltpu.roll(y, half, 1) * cm


def _dup_heads(y):
    lane = lax.broadcasted_iota(jnp.int32, (y.shape[0], LANES), 1)
    low = lane < HEAD_DIM
    out = []
    for p in range(y.shape[-1] // LANES):
        blk = y[:, p * LANES:(p + 1) * LANES]
        swapped = pltpu.roll(blk, HEAD_DIM, 1)
        out.append(jnp.where(low, blk, swapped))
        out.append(jnp.where(low, swapped, blk))
    return jnp.concatenate(out, axis=-1)


def _qkv_kernel(x_ref, g_ref, w_ref, b_ref, ca_ref, cp_ref, cm_ref, q_ref, k_ref, v_ref, h_ref):
    n = pl.program_id(1)
    n_q = QW // TN

    @pl.when(n == 0)
    def _():
        h_ref[...] = _rms(x_ref[...], g_ref[...]).astype(BF16)

    y = _dot(h_ref[...], w_ref[...]) + b_ref[...]

    @pl.when(n < n_q)
    def _():
        r = _rope(y, ca_ref[...], cp_ref[...], cm_ref[...])
        q_ref[...] = (r * (HEAD_DIM ** -0.5)).astype(BF16)

    @pl.when(n == n_q)
    def _():
        kk = _rope(y[:, :KW], ca_ref[...], cp_ref[...], cm_ref[...])
        k_ref[...] = _dup_heads(kk).astype(BF16)
        v_ref[...] = _dup_heads(y[:, KW:]).astype(BF16)


def _qkv(x, g, w, b, ca, cp, cm, seq):
    t = x.shape[0]
    n_q = QW // TN
    pos_blocks = seq // TM
    tab = pl.BlockSpec((TM, LANES), lambda i, n: (i % pos_blocks, 0))
    kv_shape = jax.ShapeDtypeStruct((t, N_KV_HEADS * LANES), BF16)
    return pl.pallas_call(
        _qkv_kernel,
        grid=(t // TM, n_q + 1),
        in_specs=[
            pl.BlockSpec((TM, D_MODEL), lambda i, n: (i, 0)),
            pl.BlockSpec((1, D_MODEL), lambda i, n: (0, 0)),
            pl.BlockSpec((D_MODEL, TN), lambda i, n: (0, n)),
            pl.BlockSpec((1, TN), lambda i, n: (0, n)),
            tab, tab, tab,
        ],
        out_specs=[
            pl.BlockSpec((TM, TN), lambda i, n: (i, jnp.minimum(n, n_q - 1))),
            pl.BlockSpec((TM, N_KV_HEADS * LANES), lambda i, n: (i, 0)),
            pl.BlockSpec((TM, N_KV_HEADS * LANES), lambda i, n: (i, 0)),
        ],
        out_shape=[jax.ShapeDtypeStruct((t, QW), BF16), kv_shape, kv_shape],
        scratch_shapes=[pltpu.VMEM((TM, D_MODEL), BF16)],
        compiler_params=_cparams(("parallel", "arbitrary")),
        name="qkv",
    )(x, g, w, b, ca, cp, cm)


def _attn_kernel(sink_ref, q_ref, kp_ref, kc_ref, vp_ref, vc_ref, o_ref):
    nb = pl.program_id(1)
    t_idx = lax.broadcasted_iota(jnp.int32, (WINDOW, 2 * WINDOW), 0)
    s_idx = lax.broadcasted_iota(jnp.int32, (WINDOW, 2 * WINDOW), 1)
    diff = t_idx + WINDOW - s_idx
    valid = (diff >= 0) & (diff < WINDOW) & ((s_idx >= WINDOW) | (nb > 0))
    lane = lax.broadcasted_iota(jnp.int32, (WINDOW, LANES), 1)
    low = lane < HEAD_DIM
    zero = jnp.zeros((), BF16)

    for kh in range(N_KV_HEADS):
        ks = slice(kh * LANES, (kh + 1) * LANES)
        k2 = jnp.concatenate([kp_ref[:, ks], kc_ref[:, ks]], axis=0)
        v2 = jnp.concatenate([vp_ref[:, ks], vc_ref[:, ks]], axis=0)
        for pp in range(Q_PER_KV // 2):
            p = kh * (Q_PER_KV // 2) + pp
            qb = q_ref[:, p * LANES:(p + 1) * LANES]
            halves = []
            for hh in range(2):
                head = 2 * p + hh
                qm = jnp.where(low if hh == 0 else ~low, qb, zero)
                s = lax.dot_general(qm, k2, (((1,), (1,)), ((), ())),
                                    preferred_element_type=F32)
                s = jnp.where(valid, s, -jnp.inf)
                sink = sink_ref[head]
                m = jnp.maximum(jnp.max(s, axis=-1, keepdims=True), sink)
                e = jnp.exp(s - m)
                denom = jnp.sum(e, axis=-1, keepdims=True) + jnp.exp(sink - m)
                probs = (e * (1.0 / denom)).astype(BF16)
                halves.append(_dot(probs, v2))
            o_ref[:, p * LANES:(p + 1) * LANES] = jnp.where(low, halves[0], halves[1]).astype(BF16)


def _attn(sinks, q, k2, v2, batch, seq):
    t = q.shape[0]
    nblk = seq // WINDOW
    kvw = N_KV_HEADS * LANES
    cur = lambda b, n: (b * nblk + n, 0)
    prev = lambda b, n: (b * nblk + jnp.maximum(n - 1, 0), 0)
    return pl.pallas_call(
        _attn_kernel,
        grid=(batch, nblk),
        in_specs=[
            pl.BlockSpec(memory_space=pltpu.SMEM),
            pl.BlockSpec((WINDOW, QW), cur),
            pl.BlockSpec((WINDOW, kvw), prev),
            pl.BlockSpec((WINDOW, kvw), cur),
            pl.BlockSpec((WINDOW, kvw), prev),
            pl.BlockSpec((WINDOW, kvw), cur),
        ],
        out_specs=pl.BlockSpec((WINDOW, QW), cur),
        out_shape=jax.ShapeDtypeStruct((t, QW), BF16),
        compiler_params=_cparams(("parallel", "arbitrary")),
        name="attn",
    )(sinks, q, k2, k2, v2, v2)


def _wo_kernel(x_ref, a_ref, w_ref, b_ref, o_ref):
    o_ref[...] = x_ref[...] + _dot(a_ref[...], w_ref[...]) + b_ref[...]


def _wo(x, a, w, b):
    t = x.shape[0]
    return pl.pallas_call(
        _wo_kernel,
        grid=(t // TM,),
        in_specs=[
            pl.BlockSpec((TM, D_MODEL), lambda i: (i, 0)),
            pl.BlockSpec((TM, QW), lambda i: (i, 0)),
            pl.BlockSpec((QW, D_MODEL), lambda i: (0, 0)),
            pl.BlockSpec((1, D_MODEL), lambda i: (0, 0)),
        ],
        out_specs=pl.BlockSpec((TM, D_MODEL), lambda i: (i, 0)),
        out_shape=jax.ShapeDtypeStruct((t, D_MODEL), F32),
        compiler_params=_cparams(("parallel",)),
        name="wo",
    )(x, a, w, b)


def _ple_kernel(x_ref, p_ref, g_ref, wg_ref, wp_ref, fg_ref, o_ref, *, final):
    xf = x_ref[...]
    h = _rms(xf, g_ref[...]).astype(BF16)
    gate = jax.nn.sigmoid(_dot(h, wg_ref[...]))
    proj = _dot(p_ref[...].astype(BF16), wp_ref[...])
    y = xf + gate * proj
    if final:
        y = _rms(y, fg_ref[...])
    o_ref[...] = y


def _ple(x, p, g, wg, wp, fg, final):
    t = x.shape[0]
    vec = pl.BlockSpec((1, D_MODEL), lambda i: (0, 0))
    return pl.pallas_call(
        functools.partial(_ple_kernel, final=final),
        grid=(t // TM_PLE,),
        in_specs=[
            pl.BlockSpec((TM_PLE, D_MODEL), lambda i: (i, 0)),
            pl.BlockSpec((TM_PLE, PLE_DIM), lambda i: (i, 0)),
            vec,
            pl.BlockSpec((D_MODEL, D_MODEL), lambda i: (0, 0)),
            pl.BlockSpec((PLE_DIM, D_MODEL), lambda i: (0, 0)),
            vec,
        ],
        out_specs=pl.BlockSpec((TM_PLE, D_MODEL), lambda i: (i, 0)),
        out_shape=jax.ShapeDtypeStruct((t, D_MODEL), F32),
        compiler_params=_cparams(("parallel",)),
        name="ple_final" if final else "ple",
    )(x, p, g, wg, wp, fg)


def _rope_coefficients(seq):
    half = ROPE_DIM // 2
    inv_freq = ROPE_THETA ** (-jnp.arange(0, ROPE_DIM, 2, dtype=F32) / ROPE_DIM)
    ang = jnp.arange(seq, dtype=F32)[:, None] * inv_freq[None, :]
    cos, sin = jnp.cos(ang), jnp.sin(ang)
    ones = jnp.ones((seq, HEAD_DIM - ROPE_DIM), F32)
    zeros = jnp.zeros((seq, HEAD_DIM - half), F32)
    ca = jnp.concatenate([cos, cos, ones], axis=-1)
    cp = jnp.concatenate([-sin, zeros], axis=-1)
    cm = jnp.concatenate([jnp.zeros((seq, half), F32), sin, zeros[:, half:]], axis=-1)
    tile = lambda a: jnp.concatenate([a] * (LANES // HEAD_DIM), axis=-1)
    return tile(ca), tile(cp), tile(cm)


def kernel(x, p, ffn1_norm, ffn1_w1, ffn1_w3, ffn1_w2, mix_norm, ffn2_norm, ffn2_w1, ffn2_w3, ffn2_w2, ple_norm, ple_w_gate, ple_w_proj, gmlp_w_in, gmlp_ln_g, gmlp_ln_b, gmlp_w_s, gmlp_b_s, gmlp_w_out, swa_wq, swa_bq, swa_wk, swa_bk, swa_wv, swa_bv, swa_sinks, swa_wo, swa_bo, final_norm):
    batch, seq, _ = x.shape
    t = batch * seq
    assert seq % TM == 0 and TM % WINDOW == 0 and t % TM_PLE == 0
    bf = lambda a: a.astype(BF16)
    vec = lambda a: a.reshape(1, -1)
    xs = x.reshape(t, D_MODEL)
    ps = p.reshape(DEPTH, t, PLE_DIM)
    ca, cp, cm = _rope_coefficients(seq)

    for i in range(DEPTH):
        xs = _ffn(xs, vec(ffn1_norm[i]), bf(ffn1_w1[i]), bf(ffn1_w3[i]), bf(ffn1_w2[i]))
        j = i // 2
        if i % 2 == 0:
            b_s = jnp.broadcast_to(gmlp_b_s[j][:, :, None], (GMLP_GROUPS, CHUNK, GMLP_GROUP_DIM))
            xs = _gmlp(xs, vec(mix_norm[i]), bf(gmlp_w_in[j]),
                       gmlp_ln_g[j].reshape(N_HALF, 1, TN), gmlp_ln_b[j].reshape(N_HALF, 1, TN),
                       gmlp_w_s[j], b_s, bf(gmlp_w_out[j]))
        else:
            w_qkv = bf(jnp.concatenate([swa_wq[j], swa_wk[j], swa_wv[j]], axis=1))
            b_qkv = vec(jnp.concatenate([swa_bq[j], swa_bk[j], swa_bv[j]]))
            q, k2, v2 = _qkv(xs, vec(mix_norm[i]), w_qkv, b_qkv, ca, cp, cm, seq)
            a = _attn(swa_sinks[j], q, k2, v2, batch, seq)
            xs = _wo(xs, a, bf(swa_wo[j]), vec(swa_bo[j]))
        xs = _ffn(xs, vec(ffn2_norm[i]), bf(ffn2_w1[i]), bf(ffn2_w3[i]), bf(ffn2_w2[i]))
        xs = _ple(xs, ps[i], vec(ple_norm[i]), bf(ple_w_gate[i]), bf(ple_w_proj[i]),
                  vec(final_norm), final=(i == DEPTH - 1))
    return xs.reshape(batch, seq, D_MODEL)
```

```python
import functools
import math

import jax
import jax.numpy as jnp
from jax import lax
from jax.experimental import pallas as pl
from jax.experimental.pallas import tpu as pltpu

F32 = jnp.float32
BF16 = jnp.bfloat16

D_MODEL = 2048
DEPTH = 2
D_FF = 5632
PLE_DIM = 256
RMS_EPS = 1e-6
LN_EPS = 1e-5
CHUNK = 128
GMLP_WIDTH = 2 * D_MODEL
GMLP_GROUPS = 16
GMLP_GROUP_DIM = GMLP_WIDTH // GMLP_GROUPS
N_Q_HEADS = 32
N_KV_HEADS = 4
HEAD_DIM = 64
Q_PER_KV = N_Q_HEADS // N_KV_HEADS
WINDOW = 128
ROPE_THETA = 500000.0
ROPE_DIM = HEAD_DIM // 4
QW = N_Q_HEADS * HEAD_DIM
KW = N_KV_HEADS * HEAD_DIM

LANES = 128

TM = 512
TM_FFN = 1024
TM_PLE = 256
TF = 512
TN = 512
N_HALF = GMLP_WIDTH // TN
GROUPS_PER_SLAB = TN // GMLP_GROUP_DIM
CAST_BLOCK_BYTES = 6 * 1024 * 1024
VMEM_LIMIT = 58 * 1024 * 1024


def _cparams(sem):
    return pltpu.CompilerParams(dimension_semantics=sem, vmem_limit_bytes=VMEM_LIMIT)


def _dot(a, b):
    return jnp.dot(a, b, preferred_element_type=F32)


def _rms(xf, g):
    ms = jnp.mean(xf * xf, axis=-1, keepdims=True)
    return xf * lax.rsqrt(ms + RMS_EPS) * g


def _gelu(z):
    return 0.5 * z * (1.0 + lax.erf(z * (1.0 / math.sqrt(2.0))))


def _cast_kernel(*refs):
    *in_refs, o_ref = refs
    col = 0
    for r in in_refs:
        w = r.shape[-1]
        o_ref[:, col:col + w] = r[...].astype(BF16)
        col += w


def _cast_bf16(*ws):
    layers, rows = ws[0].shape[:2]
    cols = sum(w.shape[2] for w in ws)
    rb = rows
    while rb * cols * 4 > CAST_BLOCK_BYTES and rb % 32 == 0:
        rb //= 2
    return pl.pallas_call(
        _cast_kernel,
        grid=(layers, rows // rb),
        in_specs=[pl.BlockSpec((None, rb, w.shape[2]), lambda l, r: (l, r, 0)) for w in ws],
        out_specs=pl.BlockSpec((None, rb, cols), lambda l, r: (l, r, 0)),
        out_shape=jax.ShapeDtypeStruct((layers, rows, cols), BF16),
        compiler_params=_cparams(("parallel", "parallel")),
        name="cast",
    )(*ws)


def _ffn_kernel(x_ref, g_ref, w1_ref, w3_ref, w2_ref, o_ref, h_ref):
    f = pl.program_id(1)

    @pl.when(f == 0)
    def _():
        xf = x_ref[...]
        h_ref[...] = _rms(xf, g_ref[...]).astype(BF16)
        o_ref[...] = xf

    h = h_ref[...]
    a = _dot(h, w1_ref[...])
    b = _dot(h, w3_ref[...])
    gate = (0.5 * (a * jax.nn.sigmoid(a)) * b).astype(BF16)
    o_ref[...] += _dot(gate, w2_ref[...])


def _ffn(x, g, w1, w3, w2, layer):
    t = x.shape[0]
    return pl.pallas_call(
        _ffn_kernel,
        grid=(t // TM_FFN, D_FF // TF),
        in_specs=[
            pl.BlockSpec((TM_FFN, D_MODEL), lambda i, f: (i, 0)),
            pl.BlockSpec((None, 1, D_MODEL), lambda i, f: (layer, 0, 0)),
            pl.BlockSpec((None, D_MODEL, TF), lambda i, f: (layer, 0, f)),
            pl.BlockSpec((None, D_MODEL, TF), lambda i, f: (layer, 0, f)),
            pl.BlockSpec((None, TF, D_MODEL), lambda i, f: (layer, f, 0)),
        ],
        out_specs=pl.BlockSpec((TM_FFN, D_MODEL), lambda i, f: (i, 0)),
        out_shape=jax.ShapeDtypeStruct((t, D_MODEL), F32),
        scratch_shapes=[pltpu.VMEM((TM_FFN, D_MODEL), BF16)],
        compiler_params=_cparams(("parallel", "arbitrary")),
        name="ffn",
    )(x, g, w1, w3, w2)


def _gmlp_kernel(x_ref, g_ref, win_ref, lng_ref, lnb_ref, ws_ref, bs_ref, wout_ref,
                 o_ref, h_ref, v_ref, s1_ref, s2_ref, mu_ref, rs_ref, gt0_ref, gt1_ref):
    j = pl.program_id(1)
    gt_refs = (gt0_ref, gt1_ref)

    @pl.when(j == 0)
    def _():
        h_ref[...] = _rms(x_ref[...], g_ref[...]).astype(BF16)
        s1_ref[...] = jnp.zeros_like(s1_ref)
        s2_ref[...] = jnp.zeros_like(s2_ref)

    @pl.when(j < N_HALF)
    def _():
        z = _gelu(_dot(h_ref[...], win_ref[...]))
        v_ref[j] = z
        p1 = z[:, :LANES]
        p2 = p1 * p1
        for c in range(1, TN // LANES):
            zc = z[:, c * LANES:(c + 1) * LANES]
            p1 += zc
            p2 += zc * zc
        s1_ref[...] += p1
        s2_ref[...] += p2

    @pl.when(j == N_HALF - 1)
    def _():
        mu = jnp.sum(s1_ref[...], axis=-1, keepdims=True) * (1.0 / GMLP_WIDTH)
        ex2 = jnp.sum(s2_ref[...], axis=-1, keepdims=True) * (1.0 / GMLP_WIDTH)
        mu_ref[...] = mu
        rs_ref[...] = lax.rsqrt(ex2 - mu * mu + LN_EPS)

    def gate(k, dst_ref):
        u = _gelu(_dot(h_ref[...], win_ref[...]))
        vn = ((v_ref[k] - mu_ref[...]) * rs_ref[...] * lng_ref[k] + lnb_ref[k]).astype(BF16)
        row = lax.broadcasted_iota(jnp.int32, (CHUNK, CHUNK), 0)
        col = lax.broadcasted_iota(jnp.int32, (CHUNK, CHUNK), 1)
        causal = row >= col
        for gg in range(GROUPS_PER_SLAB):
            grp = GROUPS_PER_SLAB * k + gg
            w = jnp.where(causal, ws_ref[grp], 0.0).astype(BF16)
            bias = bs_ref[grp]
            cols = slice(gg * GMLP_GROUP_DIM, (gg + 1) * GMLP_GROUP_DIM)
            for c in range(TM // CHUNK):
                rows = slice(c * CHUNK, (c + 1) * CHUNK)
                s = _dot(w, vn[rows, cols]) + bias
                dst_ref[rows, cols] = (u[rows, cols] * s).astype(BF16)

    def project(src_ref):
        o_ref[...] += _dot(src_ref[...], wout_ref[...])

    @pl.when(j == N_HALF)
    def _():
        o_ref[...] = x_ref[...]
        gate(0, gt_refs[0])

    for parity in range(2):
        @pl.when((j > N_HALF) & (j < 2 * N_HALF) & (((j - N_HALF) & 1) == parity))
        def _():
            project(gt_refs[1 - parity])
            gate(j - N_HALF, gt_refs[parity])

    @pl.when(j == 2 * N_HALF)
    def _():
        project(gt_refs[(N_HALF - 1) % 2])


def _gmlp(x, g, w_in, ln_g, ln_b, w_s, b_s, w_out, layer, mixer):
    t = x.shape[0]

    def win_map(i, j):
        return (mixer, 0, jnp.where(j < N_HALF, j + N_HALF, jnp.minimum(j, 2 * N_HALF - 1) - N_HALF))

    def wout_map(i, j):
        return (mixer, jnp.clip(j - N_HALF - 1, 0, N_HALF - 1), 0)

    return pl.pallas_call(
        _gmlp_kernel,
        grid=(t // TM, 2 * N_HALF + 1),
        in_specs=[
            pl.BlockSpec((TM, D_MODEL), lambda i, j: (i, 0)),
            pl.BlockSpec((None, 1, D_MODEL), lambda i, j: (layer, 0, 0)),
            pl.BlockSpec((None, D_MODEL, TN), win_map),
            pl.BlockSpec((N_HALF, 1, TN), lambda i, j: (0, 0, 0)),
            pl.BlockSpec((N_HALF, 1, TN), lambda i, j: (0, 0, 0)),
            pl.BlockSpec((None, GMLP_GROUPS, CHUNK, CHUNK), lambda i, j: (mixer, 0, 0, 0)),
            pl.BlockSpec((GMLP_GROUPS, CHUNK, GMLP_GROUP_DIM), lambda i, j: (0, 0, 0)),
            pl.BlockSpec((None, TN, D_MODEL), wout_map),
        ],
        out_specs=pl.BlockSpec((TM, D_MODEL), lambda i, j: (i, 0)),
        out_shape=jax.ShapeDtypeStruct((t, D_MODEL), F32),
        scratch_shapes=[
            pltpu.VMEM((TM, D_MODEL), BF16),
            pltpu.VMEM((N_HALF, TM, TN), F32),
            pltpu.VMEM((TM, LANES), F32),
            pltpu.VMEM((TM, LANES), F32),
            pltpu.VMEM((TM, 1), F32),
            pltpu.VMEM((TM, 1), F32),
            pltpu.VMEM((TM, TN), BF16),
            pltpu.VMEM((TM, TN), BF16),
        ],
        compiler_params=_cparams(("parallel", "arbitrary")),
        name="gmlp",
    )(x, g, w_in, ln_g, ln_b, w_s, b_s, w_out)


def _rope(y, ca, cp, cm):
    n = y.shape[-1]
    reps = n // LANES
    ca = jnp.concatenate([ca] * reps, axis=-1)
    cp = jnp.concatenate([cp] * reps, axis=-1)
    cm = jnp.concatenate([cm] * reps, axis=-1)
    half = ROPE_DIM // 2
    return y * ca + pltpu.roll(y, n - half, 1) * cp + pltpu.roll(y, half, 1) * cm


def _dup_heads(y):
    lane = lax.broadcasted_iota(jnp.int32, (y.shape[0], LANES), 1)
    low = lane < HEAD_DIM
    out = []
    for p in range(y.shape[-1] // LANES):
        blk = y[:, p * LANES:(p + 1) * LANES]
        swapped = pltpu.roll(blk, HEAD_DIM, 1)
        out.append(jnp.where(low, blk, swapped))
        out.append(jnp.where(low, swapped, blk))
    return jnp.concatenate(out, axis=-1)


def _qkv_kernel(x_ref, g_ref, w_ref, b_ref, ca_ref, cp_ref, cm_ref, q_ref, k_ref, v_ref, h_ref):
    n = pl.program_id(1)
    n_q = QW // TN

    @pl.when(n == 0)
    def _():
        h_ref[...] = _rms(x_ref[...], g_ref[...]).astype(BF16)

    @pl.when(n < n_q)
    def _():
        y = _dot(h_ref[...], w_ref[...]) + b_ref[...]
        r = _rope(y, ca_ref[...], cp_ref[...], cm_ref[...])
        q_ref[...] = (r * (HEAD_DIM ** -0.5)).astype(BF16)

    @pl.when(n == n_q)
    def _():
        y = _dot(h_ref[...], w_ref[...]) + b_ref[...]
        kk = _rope(y[:, :KW], ca_ref[...], cp_ref[...], cm_ref[...])
        k_ref[...] = _dup_heads(kk).astype(BF16)
        v_ref[...] = _dup_heads(y[:, KW:]).astype(BF16)


def _qkv(x, g, w, b, ca, cp, cm, seq, layer, mixer):
    t = x.shape[0]
    n_q = QW // TN
    pos_blocks = seq // TM
    tab = pl.BlockSpec((TM, LANES), lambda i, n: (i % pos_blocks, 0))
    kv_shape = jax.ShapeDtypeStruct((t, N_KV_HEADS * LANES), BF16)
    return pl.pallas_call(
        _qkv_kernel,
        grid=(t // TM, n_q + 1),
        in_specs=[
            pl.BlockSpec((TM, D_MODEL), lambda i, n: (i, 0)),
            pl.BlockSpec((None, 1, D_MODEL), lambda i, n: (layer, 0, 0)),
            pl.BlockSpec((None, D_MODEL, TN), lambda i, n: (mixer, 0, n)),
            pl.BlockSpec((1, TN), lambda i, n: (0, n)),
            tab, tab, tab,
        ],
        out_specs=[
            pl.BlockSpec((TM, TN), lambda i, n: (i, jnp.minimum(n, n_q - 1))),
            pl.BlockSpec((TM, N_KV_HEADS * LANES), lambda i, n: (i, 0)),
            pl.BlockSpec((TM, N_KV_HEADS * LANES), lambda i, n: (i, 0)),
        ],
        out_shape=[jax.ShapeDtypeStruct((t, QW), BF16), kv_shape, kv_shape],
        scratch_shapes=[pltpu.VMEM((TM, D_MODEL), BF16)],
        compiler_params=_cparams(("parallel", "arbitrary")),
        name="qkv",
    )(x, g, w, b, ca, cp, cm)


def _attn_kernel(sink_ref, q_ref, kp_ref, kc_ref, vp_ref, vc_ref, o_ref):
    nb = pl.program_id(1)
    t_idx = lax.broadcasted_iota(jnp.int32, (WINDOW, 2 * WINDOW), 0)
    s_idx = lax.broadcasted_iota(jnp.int32, (WINDOW, 2 * WINDOW), 1)
    diff = t_idx + WINDOW - s_idx
    valid = (diff >= 0) & (diff < WINDOW) & ((s_idx >= WINDOW) | (nb > 0))
    lane = lax.broadcasted_iota(jnp.int32, (WINDOW, LANES), 1)
    low = lane < HEAD_DIM
    zero = jnp.zeros((), BF16)

    for kh in range(N_KV_HEADS):
        ks = slice(kh * LANES, (kh + 1) * LANES)
        k2 = jnp.concatenate([kp_ref[:, ks], kc_ref[:, ks]], axis=0)
        v2 = jnp.concatenate([vp_ref[:, ks], vc_ref[:, ks]], axis=0)
        for pp in range(Q_PER_KV // 2):
            p = kh * (Q_PER_KV // 2) + pp
            qb = q_ref[:, p * LANES:(p + 1) * LANES]
            halves = []
            for hh in range(2):
                head = 2 * p + hh
                qm = jnp.where(low if hh == 0 else ~low, qb, zero)
                s = lax.dot_general(qm, k2, (((1,), (1,)), ((), ())),
                                    preferred_element_type=F32)
                s = jnp.where(valid, s, -jnp.inf)
                sink = sink_ref[head]
                m = jnp.maximum(jnp.max(s, axis=-1, keepdims=True), sink)
                e = jnp.exp(s - m)
                denom = jnp.sum(e, axis=-1, keepdims=True) + jnp.exp(sink - m)
                halves.append(_dot(e.astype(BF16), v2) * (1.0 / denom))
            o_ref[:, p * LANES:(p + 1) * LANES] = jnp.where(low, halves[0], halves[1]).astype(BF16)


def _attn(sinks, q, k2, v2, batch, seq):
    t = q.shape[0]
    nblk = seq // WINDOW
    kvw = N_KV_HEADS * LANES
    cur = lambda b, n: (b * nblk + n, 0)
    prev = lambda b, n: (b * nblk + jnp.maximum(n - 1, 0), 0)
    return pl.pallas_call(
        _attn_kernel,
        grid=(batch, nblk),
        in_specs=[
            pl.BlockSpec(memory_space=pltpu.SMEM),
            pl.BlockSpec((WINDOW, QW), cur),
            pl.BlockSpec((WINDOW, kvw), prev),
            pl.BlockSpec((WINDOW, kvw), cur),
            pl.BlockSpec((WINDOW, kvw), prev),
            pl.BlockSpec((WINDOW, kvw), cur),
        ],
        out_specs=pl.BlockSpec((WINDOW, QW), cur),
        out_shape=jax.ShapeDtypeStruct((t, QW), BF16),
        compiler_params=_cparams(("parallel", "arbitrary")),
        name="attn",
    )(sinks, q, k2, k2, v2, v2)


def _wo_kernel(x_ref, a_ref, w_ref, b_ref, o_ref):
    o_ref[...] = x_ref[...] + _dot(a_ref[...], w_ref[...]) + b_ref[...]


def _wo(x, a, w, b, mixer):
    t = x.shape[0]
    return pl.pallas_call(
        _wo_kernel,
        grid=(t // TM,),
        in_specs=[
            pl.BlockSpec((TM, D_MODEL), lambda i: (i, 0)),
            pl.BlockSpec((TM, QW), lambda i: (i, 0)),
            pl.BlockSpec((None, QW, D_MODEL), lambda i: (mixer, 0, 0)),
            pl.BlockSpec((None, 1, D_MODEL), lambda i: (mixer, 0, 0)),
        ],
        out_specs=pl.BlockSpec((TM, D_MODEL), lambda i: (i, 0)),
        out_shape=jax.ShapeDtypeStruct((t, D_MODEL), F32),
        compiler_params=_cparams(("parallel",)),
        name="wo",
    )(x, a, w, b)


def _ple_kernel(x_ref, p_ref, g_ref, wg_ref, wp_ref, fg_ref, o_ref, *, final):
    xf = x_ref[...]
    h = _rms(xf, g_ref[...]).astype(BF16)
    gate = jax.nn.sigmoid(_dot(h, wg_ref[...]))
    proj = _dot(p_ref[...].astype(BF16), wp_ref[...])
    y = xf + gate * proj
    if final:
        y = _rms(y, fg_ref[...])
    o_ref[...] = y


def _ple(x, p, g, wg, wp, fg, layer, final):
    t = x.shape[0]
    return pl.pallas_call(
        functools.partial(_ple_kernel, final=final),
        grid=(t // TM_PLE,),
        in_specs=[
            pl.BlockSpec((TM_PLE, D_MODEL), lambda i: (i, 0)),
            pl.BlockSpec((None, TM_PLE, PLE_DIM), lambda i: (layer, i, 0)),
            pl.BlockSpec((None, 1, D_MODEL), lambda i: (layer, 0, 0)),
            pl.BlockSpec((None, D_MODEL, D_MODEL), lambda i: (layer, 0, 0)),
            pl.BlockSpec((None, PLE_DIM, D_MODEL), lambda i: (layer, 0, 0)),
            pl.BlockSpec((1, D_MODEL), lambda i: (0, 0)),
        ],
        out_specs=pl.BlockSpec((TM_PLE, D_MODEL), lambda i: (i, 0)),
        out_shape=jax.ShapeDtypeStruct((t, D_MODEL), F32),
        compiler_params=_cparams(("parallel",)),
        name="ple_final" if final else "ple",
    )(x, p, g, wg, wp, fg)


def _rope_coefficients(seq):
    half = ROPE_DIM // 2
    inv_freq = ROPE_THETA ** (-jnp.arange(0, ROPE_DIM, 2, dtype=F32) / ROPE_DIM)
    ang = jnp.arange(seq, dtype=F32)[:, None] * inv_freq[None, :]
    cos, sin = jnp.cos(ang), jnp.sin(ang)
    ones = jnp.ones((seq, HEAD_DIM - ROPE_DIM), F32)
    zeros = jnp.zeros((seq, HEAD_DIM - half), F32)
    ca = jnp.concatenate([cos, cos, ones], axis=-1)
    cp = jnp.concatenate([-sin, zeros], axis=-1)
    cm = jnp.concatenate([jnp.zeros((seq, half), F32), sin, zeros[:, half:]], axis=-1)
    tile = lambda a: jnp.concatenate([a] * (LANES // HEAD_DIM), axis=-1)
    return tile(ca), tile(cp), tile(cm)


def kernel(x, p, ffn1_norm, ffn1_w1, ffn1_w3, ffn1_w2, mix_norm, ffn2_norm, ffn2_w1, ffn2_w3, ffn2_w2, ple_norm, ple_w_gate, ple_w_proj, gmlp_w_in, gmlp_ln_g, gmlp_ln_b, gmlp_w_s, gmlp_b_s, gmlp_w_out, swa_wq, swa_bq, swa_wk, swa_bk, swa_wv, swa_bv, swa_sinks, swa_wo, swa_bo, final_norm):
    batch, seq, _ = x.shape
    t = batch * seq
    assert seq % TM == 0 and TM % WINDOW == 0 and t % TM_FFN == 0 and t % TM_PLE == 0
    rows = lambda a: a.reshape(a.shape[0], 1, a.shape[1])
    xs = x.reshape(t, D_MODEL)
    ps = p.reshape(DEPTH, t, PLE_DIM)
    ca, cp, cm = _rope_coefficients(seq)

    ffn1 = (rows(ffn1_norm), _cast_bf16(ffn1_w1), _cast_bf16(ffn1_w3), _cast_bf16(ffn1_w2))
    ffn2 = (rows(ffn2_norm), _cast_bf16(ffn2_w1), _cast_bf16(ffn2_w3), _cast_bf16(ffn2_w2))
    mix_g = rows(mix_norm)
    ple_g, ple_wg, ple_wp = rows(ple_norm), _cast_bf16(ple_w_gate), _cast_bf16(ple_w_proj)
    gmlp_win, gmlp_wout = _cast_bf16(gmlp_w_in), _cast_bf16(gmlp_w_out)
    swa_wqkv, swa_wo_b = _cast_bf16(swa_wq, swa_wk, swa_wv), _cast_bf16(swa_wo)
    swa_bqkv = jnp.concatenate([swa_bq, swa_bk, swa_bv], axis=1)
    fg = final_norm.reshape(1, D_MODEL)

    for i in range(DEPTH):
        xs = _ffn(xs, *ffn1, layer=i)
        j = i // 2
        if i % 2 == 0:
            b_s = jnp.broadcast_to(gmlp_b_s[j][:, :, None], (GMLP_GROUPS, CHUNK, GMLP_GROUP_DIM))
            xs = _gmlp(xs, mix_g, gmlp_win,
                       gmlp_ln_g[j].reshape(N_HALF, 1, TN), gmlp_ln_b[j].reshape(N_HALF, 1, TN),
                       gmlp_w_s, b_s, gmlp_wout, layer=i, mixer=j)
        else:
            q, k2, v2 = _qkv(xs, mix_g, swa_wqkv, swa_bqkv[j:j + 1], ca, cp, cm, seq, layer=i, mixer=j)
            a = _attn(swa_sinks[j], q, k2, v2, batch, seq)
            xs = _wo(xs, a, swa_wo_b, rows(swa_bo), mixer=j)
        xs = _ffn(xs, *ffn2, layer=i)
        xs = _ple(xs, ps, ple_g, ple_wg, ple_wp, fg, layer=i, final=(i == DEPTH - 1))
    return xs.reshape(batch, seq, D_MODEL)
```

```python
import functools
import math

import jax
import jax.numpy as jnp
import numpy as np
from jax import lax
from jax.experimental import pallas as pl
from jax.experimental.pallas import tpu as pltpu

F32 = jnp.float32
BF16 = jnp.bfloat16

D_MODEL = 2048
DEPTH = 2
D_FF = 5632
PLE_DIM = 256
RMS_EPS = 1e-6
LN_EPS = 1e-5
CHUNK = 128
GMLP_WIDTH = 2 * D_MODEL
GMLP_GROUPS = 16
GMLP_GROUP_DIM = GMLP_WIDTH // GMLP_GROUPS
N_Q_HEADS = 32
N_KV_HEADS = 4
HEAD_DIM = 64
Q_PER_KV = N_Q_HEADS // N_KV_HEADS
WINDOW = 128
ROPE_THETA = 500000.0
ROPE_DIM = HEAD_DIM // 4
QW = N_Q_HEADS * HEAD_DIM
KW = N_KV_HEADS * HEAD_DIM

LANES = 128

TM = 512
TM_FFN = 1024
TF = 512
TN = 512
TG = 1024
N_HALF = GMLP_WIDTH // TG
GROUPS_PER_SLAB = TG // GMLP_GROUP_DIM
CAST_BLOCK_BYTES = 6 * 1024 * 1024
VMEM_LIMIT = 58 * 1024 * 1024


def _cparams(sem):
    return pltpu.CompilerParams(dimension_semantics=sem, vmem_limit_bytes=VMEM_LIMIT)


def _dot(a, b):
    return jnp.dot(a, b, preferred_element_type=F32)


def _rms(xf, g):
    ms = jnp.mean(xf * xf, axis=-1, keepdims=True)
    return xf * lax.rsqrt(ms + RMS_EPS) * g


def _gelu(z):
    return 0.5 * z * (1.0 + lax.erf(z * (1.0 / math.sqrt(2.0))))


def _cast_kernel(*refs):
    *in_refs, o_ref = refs
    col = 0
    for r in in_refs:
        w = r.shape[-1]
        o_ref[:, col:col + w] = r[...].astype(BF16)
        col += w


def _cast_bf16(*ws):
    layers, rows = ws[0].shape[:2]
    cols = sum(w.shape[2] for w in ws)
    rb = rows
    while rb * cols * 4 > CAST_BLOCK_BYTES and rb % 32 == 0:
        rb //= 2
    return pl.pallas_call(
        _cast_kernel,
        grid=(layers, rows // rb),
        in_specs=[pl.BlockSpec((None, rb, w.shape[2]), lambda l, r: (l, r, 0)) for w in ws],
        out_specs=pl.BlockSpec((None, rb, cols), lambda l, r: (l, r, 0)),
        out_shape=jax.ShapeDtypeStruct((layers, rows, cols), BF16),
        compiler_params=_cparams(("parallel", "parallel")),
        name="cast",
    )(*ws)


def _ffn_kernel(x_ref, g_ref, w1_ref, w3_ref, w2_ref, o_ref, h_ref):
    f = pl.program_id(1)

    def swiglu(h):
        a = _dot(h, w1_ref[...])
        b = _dot(h, w3_ref[...])
        gate = (0.5 * (a * jax.nn.sigmoid(a)) * b).astype(BF16)
        return _dot(gate, w2_ref[...])

    @pl.when(f == 0)
    def _():
        xf = x_ref[...]
        h = _rms(xf, g_ref[...]).astype(BF16)
        h_ref[...] = h
        o_ref[...] = xf + swiglu(h)

    @pl.when(f > 0)
    def _():
        o_ref[...] += swiglu(h_ref[...])


def _ffn(x, g, w1, w3, w2, layer):
    t = x.shape[0]
    return pl.pallas_call(
        _ffn_kernel,
        grid=(t // TM_FFN, D_FF // TF),
        in_specs=[
            pl.BlockSpec((TM_FFN, D_MODEL), lambda i, f: (i, 0)),
            pl.BlockSpec((None, 1, D_MODEL), lambda i, f: (layer, 0, 0)),
            pl.BlockSpec((None, D_MODEL, TF), lambda i, f: (layer, 0, f)),
            pl.BlockSpec((None, D_MODEL, TF), lambda i, f: (layer, 0, f)),
            pl.BlockSpec((None, TF, D_MODEL), lambda i, f: (layer, f, 0)),
        ],
        out_specs=pl.BlockSpec((TM_FFN, D_MODEL), lambda i, f: (i, 0)),
        out_shape=jax.ShapeDtypeStruct((t, D_MODEL), F32),
        scratch_shapes=[pltpu.VMEM((TM_FFN, D_MODEL), BF16)],
        compiler_params=_cparams(("parallel", "arbitrary")),
        name="ffn",
    )(x, g, w1, w3, w2)


def _gmlp_kernel(x_ref, g_ref, win_ref, lng_ref, lnb_ref, ws_ref, bs_ref, wout_ref,
                 o_ref, h_ref, v_ref, s1_ref, s2_ref, mu_ref, rs_ref, gt0_ref, gt1_ref):
    j = pl.program_id(1)
    gt_refs = (gt0_ref, gt1_ref)

    @pl.when(j == 0)
    def _():
        h_ref[...] = _rms(x_ref[...], g_ref[...]).astype(BF16)
        s1_ref[...] = jnp.zeros_like(s1_ref)
        s2_ref[...] = jnp.zeros_like(s2_ref)

    @pl.when(j < N_HALF)
    def _():
        z = _gelu(_dot(h_ref[...], win_ref[...]))
        v_ref[j] = z
        p1 = z[:, :LANES]
        p2 = p1 * p1
        for c in range(1, TG // LANES):
            zc = z[:, c * LANES:(c + 1) * LANES]
            p1 += zc
            p2 += zc * zc
        s1_ref[...] += p1
        s2_ref[...] += p2

    @pl.when(j == N_HALF - 1)
    def _():
        mu = jnp.sum(s1_ref[...], axis=-1, keepdims=True) * (1.0 / GMLP_WIDTH)
        ex2 = jnp.sum(s2_ref[...], axis=-1, keepdims=True) * (1.0 / GMLP_WIDTH)
        mu_ref[...] = mu
        rs_ref[...] = lax.rsqrt(ex2 - mu * mu + LN_EPS)

    def gate(k, dst_ref):
        u = _gelu(_dot(h_ref[...], win_ref[...]))
        vn = ((v_ref[k] - mu_ref[...]) * rs_ref[...] * lng_ref[k] + lnb_ref[k]).astype(BF16)
        row = lax.broadcasted_iota(jnp.int32, (CHUNK, CHUNK), 0)
        col = lax.broadcasted_iota(jnp.int32, (CHUNK, CHUNK), 1)
        causal = row >= col
        for gg in range(GROUPS_PER_SLAB):
            grp = GROUPS_PER_SLAB * k + gg
            w = jnp.where(causal, ws_ref[grp], 0.0).astype(BF16)
            bias = bs_ref[grp]
            cols = slice(gg * GMLP_GROUP_DIM, (gg + 1) * GMLP_GROUP_DIM)
            for c in range(TM // CHUNK):
                rows = slice(c * CHUNK, (c + 1) * CHUNK)
                s = _dot(w, vn[rows, cols]) + bias
                dst_ref[rows, cols] = (u[rows, cols] * s).astype(BF16)

    def project(src_ref):
        o_ref[...] += _dot(src_ref[...], wout_ref[...])

    @pl.when(j == N_HALF)
    def _():
        o_ref[...] = x_ref[...]
        gate(0, gt_refs[0])

    for parity in range(2):
        @pl.when((j > N_HALF) & (j < 2 * N_HALF) & (((j - N_HALF) & 1) == parity))
        def _():
            project(gt_refs[1 - parity])
            gate(j - N_HALF, gt_refs[parity])

    @pl.when(j == 2 * N_HALF)
    def _():
        project(gt_refs[(N_HALF - 1) % 2])


def _gmlp(x, g, w_in, ln_g, ln_b, w_s, b_s, w_out, layer, mixer):
    t = x.shape[0]

    def win_map(i, j):
        return (mixer, 0, jnp.where(j < N_HALF, j + N_HALF, jnp.minimum(j, 2 * N_HALF - 1) - N_HALF))

    def wout_map(i, j):
        return (mixer, jnp.clip(j - N_HALF - 1, 0, N_HALF - 1), 0)

    return pl.pallas_call(
        _gmlp_kernel,
        grid=(t // TM, 2 * N_HALF + 1),
        in_specs=[
            pl.BlockSpec((TM, D_MODEL), lambda i, j: (i, 0)),
            pl.BlockSpec((None, 1, D_MODEL), lambda i, j: (layer, 0, 0)),
            pl.BlockSpec((None, D_MODEL, TG), win_map),
            pl.BlockSpec((N_HALF, 1, TG), lambda i, j: (0, 0, 0)),
            pl.BlockSpec((N_HALF, 1, TG), lambda i, j: (0, 0, 0)),
            pl.BlockSpec((None, GMLP_GROUPS, CHUNK, CHUNK), lambda i, j: (mixer, 0, 0, 0)),
            pl.BlockSpec((GMLP_GROUPS, CHUNK, GMLP_GROUP_DIM), lambda i, j: (0, 0, 0)),
            pl.BlockSpec((None, TG, D_MODEL), wout_map),
        ],
        out_specs=pl.BlockSpec((TM, D_MODEL), lambda i, j: (i, 0)),
        out_shape=jax.ShapeDtypeStruct((t, D_MODEL), F32),
        scratch_shapes=[
            pltpu.VMEM((TM, D_MODEL), BF16),
            pltpu.VMEM((N_HALF, TM, TG), F32),
            pltpu.VMEM((TM, LANES), F32),
            pltpu.VMEM((TM, LANES), F32),
            pltpu.VMEM((TM, 1), F32),
            pltpu.VMEM((TM, 1), F32),
            pltpu.VMEM((TM, TG), BF16),
            pltpu.VMEM((TM, TG), BF16),
        ],
        compiler_params=_cparams(("parallel", "arbitrary")),
        name="gmlp",
    )(x, g, w_in, ln_g, ln_b, w_s, b_s, w_out)


def _rope(y, ca, cp, cm):
    n = y.shape[-1]
    reps = n // LANES
    ca = jnp.concatenate([ca] * reps, axis=-1)
    cp = jnp.concatenate([cp] * reps, axis=-1)
    cm = jnp.concatenate([cm] * reps, axis=-1)
    half = ROPE_DIM // 2
    return y * ca + pltpu.roll(y, n - half, 1) * cp + pltpu.roll(y, half, 1) * cm


def _dup_heads(y):
    lane = lax.broadcasted_iota(jnp.int32, (y.shape[0], LANES), 1)
    low = lane < HEAD_DIM
    out = []
    for p in range(y.shape[-1] // LANES):
        blk = y[:, p * LANES:(p + 1) * LANES]
        swapped = pltpu.roll(blk, HEAD_DIM, 1)
        out.append(jnp.where(low, blk, swapped))
        out.append(jnp.where(low, swapped, blk))
    return jnp.concatenate(out, axis=-1)


def _qkv_kernel(x_ref, g_ref, w_ref, b_ref, ca_ref, cp_ref, cm_ref, q_ref, k_ref, v_ref):
    h = _rms(x_ref[...], g_ref[...]).astype(BF16)
    ca, cp, cm = ca_ref[...], cp_ref[...], cm_ref[...]
    for n in range(QW // TN):
        cols = slice(n * TN, (n + 1) * TN)
        y = _dot(h, w_ref[:, cols]) + b_ref[:, cols]
        q_ref[:, cols] = (_rope(y, ca, cp, cm) * (HEAD_DIM ** -0.5)).astype(BF16)
    y = _dot(h, w_ref[:, QW:]) + b_ref[:, QW:]
    k_ref[...] = _dup_heads(_rope(y[:, :KW], ca, cp, cm)).astype(BF16)
    v_ref[...] = _dup_heads(y[:, KW:]).astype(BF16)


def _qkv(x, g, w, b, ca, cp, cm, seq, layer, mixer):
    t = x.shape[0]
    wide = QW + 2 * KW
    pos_blocks = seq // TM
    tab = pl.BlockSpec((TM, LANES), lambda i: (i % pos_blocks, 0))
    kv_shape = jax.ShapeDtypeStruct((t, N_KV_HEADS * LANES), BF16)
    return pl.pallas_call(
        _qkv_kernel,
        grid=(t // TM,),
        in_specs=[
            pl.BlockSpec((TM, D_MODEL), lambda i: (i, 0)),
            pl.BlockSpec((None, 1, D_MODEL), lambda i: (layer, 0, 0)),
            pl.BlockSpec((None, D_MODEL, wide), lambda i: (mixer, 0, 0)),
            pl.BlockSpec((None, 1, wide), lambda i: (mixer, 0, 0)),
            tab, tab, tab,
        ],
        out_specs=[
            pl.BlockSpec((TM, QW), lambda i: (i, 0)),
            pl.BlockSpec((TM, N_KV_HEADS * LANES), lambda i: (i, 0)),
            pl.BlockSpec((TM, N_KV_HEADS * LANES), lambda i: (i, 0)),
        ],
        out_shape=[jax.ShapeDtypeStruct((t, QW), BF16), kv_shape, kv_shape],
        compiler_params=_cparams(("parallel",)),
        name="qkv",
    )(x, g, w, b, ca, cp, cm)


def _attn_kernel(sink_ref, q_ref, kp_ref, kc_ref, vp_ref, vc_ref, o_ref):
    nb = pl.program_id(1)
    t_idx = lax.broadcasted_iota(jnp.int32, (WINDOW, 2 * WINDOW), 0)
    s_idx = lax.broadcasted_iota(jnp.int32, (WINDOW, 2 * WINDOW), 1)
    diff = t_idx + WINDOW - s_idx
    valid = (diff >= 0) & (diff < WINDOW) & ((s_idx >= WINDOW) | (nb > 0))
    lane = lax.broadcasted_iota(jnp.int32, (WINDOW, LANES), 1)
    low = lane < HEAD_DIM
    zero = jnp.zeros((), BF16)

    for kh in range(N_KV_HEADS):
        ks = slice(kh * LANES, (kh + 1) * LANES)
        k2 = jnp.concatenate([kp_ref[:, ks], kc_ref[:, ks]], axis=0)
        v2 = jnp.concatenate([vp_ref[:, ks], vc_ref[:, ks]], axis=0)
        for pp in range(Q_PER_KV // 2):
            p = kh * (Q_PER_KV // 2) + pp
            qb = q_ref[:, p * LANES:(p + 1) * LANES]
            halves = []
            for hh in range(2):
                head = 2 * p + hh
                qm = jnp.where(low if hh == 0 else ~low, qb, zero)
                s = lax.dot_general(qm, k2, (((1,), (1,)), ((), ())),
                                    preferred_element_type=F32)
                s = jnp.where(valid, s, -jnp.inf)
                sink = sink_ref[head]
                m = jnp.maximum(jnp.max(s, axis=-1, keepdims=True), sink)
                e = jnp.exp(s - m)
                denom = jnp.sum(e, axis=-1, keepdims=True) + jnp.exp(sink - m)
                halves.append(_dot(e.astype(BF16), v2) * (1.0 / denom))
            o_ref[:, p * LANES:(p + 1) * LANES] = jnp.where(low, halves[0], halves[1]).astype(BF16)


def _attn(sinks, q, k2, v2, batch, seq):
    t = q.shape[0]
    nblk = seq // WINDOW
    kvw = N_KV_HEADS * LANES
    cur = lambda b, n: (b * nblk + n, 0)
    prev = lambda b, n: (b * nblk + jnp.maximum(n - 1, 0), 0)
    return pl.pallas_call(
        _attn_kernel,
        grid=(batch, nblk),
        in_specs=[
            pl.BlockSpec(memory_space=pltpu.SMEM),
            pl.BlockSpec((WINDOW, QW), cur),
            pl.BlockSpec((WINDOW, kvw), prev),
            pl.BlockSpec((WINDOW, kvw), cur),
            pl.BlockSpec((WINDOW, kvw), prev),
            pl.BlockSpec((WINDOW, kvw), cur),
        ],
        out_specs=pl.BlockSpec((WINDOW, QW), cur),
        out_shape=jax.ShapeDtypeStruct((t, QW), BF16),
        compiler_params=_cparams(("parallel", "arbitrary")),
        name="attn",
    )(sinks, q, k2, k2, v2, v2)


def _wo_kernel(x_ref, a_ref, w_ref, b_ref, o_ref):
    o_ref[...] = x_ref[...] + _dot(a_ref[...], w_ref[...]) + b_ref[...]


def _wo(x, a, w, b, mixer):
    t = x.shape[0]
    return pl.pallas_call(
        _wo_kernel,
        grid=(t // TM,),
        in_specs=[
            pl.BlockSpec((TM, D_MODEL), lambda i: (i, 0)),
            pl.BlockSpec((TM, QW), lambda i: (i, 0)),
            pl.BlockSpec((None, QW, D_MODEL), lambda i: (mixer, 0, 0)),
            pl.BlockSpec((None, 1, D_MODEL), lambda i: (mixer, 0, 0)),
        ],
        out_specs=pl.BlockSpec((TM, D_MODEL), lambda i: (i, 0)),
        out_shape=jax.ShapeDtypeStruct((t, D_MODEL), F32),
        compiler_params=_cparams(("parallel",)),
        name="wo",
    )(x, a, w, b)


def _ple_kernel(x_ref, p_ref, g_ref, wg_ref, wp_ref, fg_ref, o_ref, *, final):
    xf = x_ref[...]
    h = _rms(xf, g_ref[...]).astype(BF16)
    gate = jax.nn.sigmoid(_dot(h, wg_ref[...]))
    proj = _dot(p_ref[...].astype(BF16), wp_ref[...])
    y = xf + gate * proj
    if final:
        y = _rms(y, fg_ref[...])
    o_ref[...] = y


def _ple(x, p, g, wg, wp, fg, layer, final):
    t = x.shape[0]
    return pl.pallas_call(
        functools.partial(_ple_kernel, final=final),
        grid=(t // TM,),
        in_specs=[
            pl.BlockSpec((TM, D_MODEL), lambda i: (i, 0)),
            pl.BlockSpec((None, TM, PLE_DIM), lambda i: (layer, i, 0)),
            pl.BlockSpec((None, 1, D_MODEL), lambda i: (layer, 0, 0)),
            pl.BlockSpec((None, D_MODEL, D_MODEL), lambda i: (layer, 0, 0)),
            pl.BlockSpec((None, PLE_DIM, D_MODEL), lambda i: (layer, 0, 0)),
            pl.BlockSpec((1, D_MODEL), lambda i: (0, 0)),
        ],
        out_specs=pl.BlockSpec((TM, D_MODEL), lambda i: (i, 0)),
        out_shape=jax.ShapeDtypeStruct((t, D_MODEL), F32),
        compiler_params=_cparams(("parallel",)),
        name="ple_final" if final else "ple",
    )(x, p, g, wg, wp, fg)


def _rope_coefficients(seq):
    half = ROPE_DIM // 2
    inv_freq = np.float32(ROPE_THETA) ** (-np.arange(0, ROPE_DIM, 2, dtype=np.float32) / np.float32(ROPE_DIM))
    ang = (np.arange(seq, dtype=np.float32)[:, None] * inv_freq[None, :]).astype(np.float32)
    cos = np.cos(ang.astype(np.float64)).astype(np.float32)
    sin = np.sin(ang.astype(np.float64)).astype(np.float32)
    ones = np.ones((seq, HEAD_DIM - ROPE_DIM), np.float32)
    zeros = np.zeros((seq, HEAD_DIM - half), np.float32)
    ca = np.concatenate([cos, cos, ones], axis=-1)
    cp = np.concatenate([-sin, zeros], axis=-1)
    cm = np.concatenate([np.zeros((seq, half), np.float32), sin, zeros[:, half:]], axis=-1)
    tile = lambda a: jnp.asarray(np.concatenate([a] * (LANES // HEAD_DIM), axis=-1))
    return tile(ca), tile(cp), tile(cm)


def kernel(x, p, ffn1_norm, ffn1_w1, ffn1_w3, ffn1_w2, mix_norm, ffn2_norm, ffn2_w1, ffn2_w3, ffn2_w2, ple_norm, ple_w_gate, ple_w_proj, gmlp_w_in, gmlp_ln_g, gmlp_ln_b, gmlp_w_s, gmlp_b_s, gmlp_w_out, swa_wq, swa_bq, swa_wk, swa_bk, swa_wv, swa_bv, swa_sinks, swa_wo, swa_bo, final_norm):
    batch, seq, _ = x.shape
    t = batch * seq
    assert seq % TM == 0 and TM % WINDOW == 0 and t % TM_FFN == 0
    rows = lambda a: a.reshape(a.shape[0], 1, a.shape[1])
    xs = x.reshape(t, D_MODEL)
    ps = p.reshape(DEPTH, t, PLE_DIM)
    ca, cp, cm = _rope_coefficients(seq)

    ffn1 = (rows(ffn1_norm), _cast_bf16(ffn1_w1), _cast_bf16(ffn1_w3), _cast_bf16(ffn1_w2))
    ffn2 = (rows(ffn2_norm), _cast_bf16(ffn2_w1), _cast_bf16(ffn2_w3), _cast_bf16(ffn2_w2))
    mix_g = rows(mix_norm)
    ple_g, ple_wg, ple_wp = rows(ple_norm), _cast_bf16(ple_w_gate), _cast_bf16(ple_w_proj)
    gmlp_win, gmlp_wout = _cast_bf16(gmlp_w_in), _cast_bf16(gmlp_w_out)
    swa_wqkv, swa_wo_b = _cast_bf16(swa_wq, swa_wk, swa_wv), _cast_bf16(swa_wo)
    swa_bqkv = rows(jnp.concatenate([swa_bq, swa_bk, swa_bv], axis=1))
    fg = final_norm.reshape(1, D_MODEL)

    for i in range(DEPTH):
        xs = _ffn(xs, *ffn1, layer=i)
        j = i // 2
        if i % 2 == 0:
            b_s = jnp.broadcast_to(gmlp_b_s[j][:, :, None], (GMLP_GROUPS, CHUNK, GMLP_GROUP_DIM))
            xs = _gmlp(xs, mix_g, gmlp_win,
                       gmlp_ln_g[j].reshape(N_HALF, 1, TG), gmlp_ln_b[j].reshape(N_HALF, 1, TG),
                       gmlp_w_s, b_s, gmlp_wout, layer=i, mixer=j)
        else:
            q, k2, v2 = _qkv(xs, mix_g, swa_wqkv, swa_bqkv, ca, cp, cm, seq, layer=i, mixer=j)
            a = _attn(swa_sinks[j], q, k2, v2, batch, seq)
            xs = _wo(xs, a, swa_wo_b, rows(swa_bo), mixer=j)
        xs = _ffn(xs, *ffn2, layer=i)
        xs = _ple(xs, ps, ple_g, ple_wg, ple_wp, fg, layer=i, final=(i == DEPTH - 1))
    return xs.reshape(batch, seq, D_MODEL)
```

```python
import functools
import math
from typing import NamedTuple

import jax
import jax.numpy as jnp
import numpy as np
from jax import lax
from jax.experimental import pallas as pl
from jax.experimental.pallas import tpu as pltpu

F32 = jnp.float32
BF16 = jnp.bfloat16

D_MODEL = 2048
DEPTH = 2
D_FF = 5632
PLE_DIM = 256
RMS_EPS = 1e-6
LN_EPS = 1e-5
CHUNK = 128
GMLP_WIDTH = 2 * D_MODEL
GMLP_GROUPS = 16
GMLP_GROUP_DIM = GMLP_WIDTH // GMLP_GROUPS
N_Q_HEADS = 32
N_KV_HEADS = 4
HEAD_DIM = 64
Q_PER_KV = N_Q_HEADS // N_KV_HEADS
WINDOW = 128
ROPE_THETA = 500000.0
ROPE_DIM = HEAD_DIM // 4
QW = N_Q_HEADS * HEAD_DIM
KW = N_KV_HEADS * HEAD_DIM

LANES = 128
BF16_SUBLANES = 16

TM = 512
TM_FFN = 1024
TF = 512
TN = 512
TG = 1024
N_HALF = GMLP_WIDTH // TG
GROUPS_PER_SLAB = TG // GMLP_GROUP_DIM
CAST_BLOCK_BYTES = 6 * 1024 * 1024
VMEM_LIMIT = 62 * 1024 * 1024


def _cparams(sem):
    return pltpu.CompilerParams(dimension_semantics=sem, vmem_limit_bytes=VMEM_LIMIT)


def _dot(a, b):
    return jnp.dot(a, b, preferred_element_type=F32)


def _rms(xf, g):
    ms = jnp.mean(xf * xf, axis=-1, keepdims=True)
    return xf * lax.rsqrt(ms + RMS_EPS) * g


def _gelu(z):
    return 0.5 * z * (1.0 + lax.erf(z * (1.0 / math.sqrt(2.0))))


def _cast_kernel(*refs):
    *in_refs, o_ref = refs
    col = 0
    for r in in_refs:
        w = r.shape[-1]
        o_ref[:, col:col + w] = r[...].astype(BF16)
        col += w


def _cast_bf16(layer, *ws):
    rows = ws[0].shape[1]
    cols = sum(w.shape[2] for w in ws)
    rb = rows
    while rb * cols * 4 > CAST_BLOCK_BYTES and rb % 32 == 0:
        rb //= 2
    return pl.pallas_call(
        _cast_kernel,
        grid=(rows // rb,),
        in_specs=[pl.BlockSpec((None, rb, w.shape[2]), lambda r: (layer, r, 0)) for w in ws],
        out_specs=pl.BlockSpec((rb, cols), lambda r: (r, 0)),
        out_shape=jax.ShapeDtypeStruct((rows, cols), BF16),
        compiler_params=_cparams(("parallel",)),
        name="cast",
    )(*ws)


class _CastJob(NamedTuple):
    layer: int
    arrays: tuple


def _cast_job_blocking(rows, n_tiles, n_steps):
    per_tile = rows // n_tiles
    for n_sub in range(n_steps, 0, -1):
        if per_tile % n_sub == 0 and (per_tile // n_sub) % BF16_SUBLANES == 0:
            return per_tile // n_sub, n_sub
    raise ValueError(f"cannot spread a {rows}-row cast over {n_tiles} tiles")


def _ffn_kernel(*refs, job_sizes):
    n_cast_in = sum(job_sizes)
    x_ref, g_ref, w1_ref, w3_ref, w2_ref = refs[:5]
    cast_in = refs[5:5 + n_cast_in]
    o_ref = refs[5 + n_cast_in]
    cast_out = refs[6 + n_cast_in:6 + n_cast_in + len(job_sizes)]
    h_ref = refs[-1]
    f = pl.program_id(1)

    def swiglu(h):
        a = _dot(h, w1_ref[...])
        b = _dot(h, w3_ref[...])
        gate = (0.5 * (a * jax.nn.sigmoid(a)) * b).astype(BF16)
        return _dot(gate, w2_ref[...])

    def side_casts():
        start = 0
        for n, out_ref in zip(job_sizes, cast_out):
            _cast_kernel(*cast_in[start:start + n], out_ref)
            start += n

    @pl.when(f == 0)
    def _():
        xf = x_ref[...]
        h = _rms(xf, g_ref[...]).astype(BF16)
        h_ref[...] = h
        o_ref[...] = xf + swiglu(h)
        side_casts()

    @pl.when(f > 0)
    def _():
        o_ref[...] += swiglu(h_ref[...])
        side_casts()


def _ffn(x, g, w1, w3, w2, layer, jobs):
    t = x.shape[0]
    n_tiles, n_steps = t // TM_FFN, D_FF // TF
    in_specs = [
        pl.BlockSpec((TM_FFN, D_MODEL), lambda i, f: (i, 0)),
        pl.BlockSpec((None, 1, D_MODEL), lambda i, f: (layer, 0, 0)),
        pl.BlockSpec((D_MODEL, TF), lambda i, f: (0, f)),
        pl.BlockSpec((D_MODEL, TF), lambda i, f: (0, f)),
        pl.BlockSpec((TF, D_MODEL), lambda i, f: (f, 0)),
    ]
    out_specs = [pl.BlockSpec((TM_FFN, D_MODEL), lambda i, f: (i, 0))]
    out_shape = [jax.ShapeDtypeStruct((t, D_MODEL), F32)]
    cast_inputs = []
    for job in jobs:
        rows = job.arrays[0].shape[1]
        cols = sum(a.shape[2] for a in job.arrays)
        rb, n_sub = _cast_job_blocking(rows, n_tiles, n_steps)
        block = lambda i, f, n_sub=n_sub: i * n_sub + jnp.minimum(f, n_sub - 1)
        for a in job.arrays:
            in_specs.append(pl.BlockSpec(
                (None, rb, a.shape[2]),
                lambda i, f, block=block, l=job.layer: (l, block(i, f), 0)))
            cast_inputs.append(a)
        out_specs.append(pl.BlockSpec((rb, cols), lambda i, f, block=block: (block(i, f), 0)))
        out_shape.append(jax.ShapeDtypeStruct((rows, cols), BF16))
    y, *cast = pl.pallas_call(
        functools.partial(_ffn_kernel, job_sizes=tuple(len(job.arrays) for job in jobs)),
        grid=(n_tiles, n_steps),
        in_specs=in_specs,
        out_specs=out_specs,
        out_shape=out_shape,
        scratch_shapes=[pltpu.VMEM((TM_FFN, D_MODEL), BF16)],
        compiler_params=_cparams(("arbitrary", "arbitrary")),
        name="ffn",
    )(x, g, w1, w3, w2, *cast_inputs)
    return y, cast


def _gmlp_kernel(x_ref, g_ref, win_ref, lng_ref, lnb_ref, ws_ref, bs_ref, wout_ref,
                 o_ref, h_ref, v_ref, s1_ref, s2_ref, mu_ref, rs_ref, gt0_ref, gt1_ref):
    j = pl.program_id(1)
    gt_refs = (gt0_ref, gt1_ref)

    def v_slab(h):
        z = _gelu(_dot(h, win_ref[...]))
        v_ref[j] = z
        p1 = z[:, :LANES]
        p2 = p1 * p1
        for c in range(1, TG // LANES):
            zc = z[:, c * LANES:(c + 1) * LANES]
            p1 += zc
            p2 += zc * zc
        return p1, p2

    @pl.when(j == 0)
    def _():
        h = _rms(x_ref[...], g_ref[...]).astype(BF16)
        h_ref[...] = h
        s1_ref[...], s2_ref[...] = v_slab(h)

    @pl.when((j > 0) & (j < N_HALF))
    def _():
        p1, p2 = v_slab(h_ref[...])
        s1_ref[...] += p1
        s2_ref[...] += p2

    def gate(k, dst_ref):
        u = _gelu(_dot(h_ref[...], win_ref[...]))
        if isinstance(k, int) and k == 0:
            mu = jnp.sum(s1_ref[...], axis=-1, keepdims=True) * (1.0 / GMLP_WIDTH)
            ex2 = jnp.sum(s2_ref[...], axis=-1, keepdims=True) * (1.0 / GMLP_WIDTH)
            rs = lax.rsqrt(ex2 - mu * mu + LN_EPS)
            mu_ref[...] = mu
            rs_ref[...] = rs
        else:
            mu, rs = mu_ref[...], rs_ref[...]
        vn = ((v_ref[k] - mu) * rs * lng_ref[k] + lnb_ref[k]).astype(BF16)
        row = lax.broadcasted_iota(jnp.int32, (CHUNK, CHUNK), 0)
        col = lax.broadcasted_iota(jnp.int32, (CHUNK, CHUNK), 1)
        causal = row >= col
        for gg in range(GROUPS_PER_SLAB):
            grp = GROUPS_PER_SLAB * k + gg
            w = jnp.where(causal, ws_ref[grp], 0.0).astype(BF16)
            bias = bs_ref[grp]
            cols = slice(gg * GMLP_GROUP_DIM, (gg + 1) * GMLP_GROUP_DIM)
            for c in range(TM // CHUNK):
                rows = slice(c * CHUNK, (c + 1) * CHUNK)
                s = _dot(w, vn[rows, cols]) + bias
                dst_ref[rows, cols] = (u[rows, cols] * s).astype(BF16)

    def project(src_ref):
        o_ref[...] += _dot(src_ref[...], wout_ref[...])

    @pl.when(j == N_HALF)
    def _():
        o_ref[...] = x_ref[...]
        gate(0, gt_refs[0])

    for parity in range(2):
        @pl.when((j > N_HALF) & (j < 2 * N_HALF) & (((j - N_HALF) & 1) == parity))
        def _():
            project(gt_refs[1 - parity])
            gate(j - N_HALF, gt_refs[parity])

    @pl.when(j == 2 * N_HALF)
    def _():
        project(gt_refs[(N_HALF - 1) % 2])


def _gmlp(x, g, w_in, ln_g, ln_b, w_s, b_s, w_out, layer, mixer):
    t = x.shape[0]

    def win_map(i, j):
        return (0, jnp.where(j < N_HALF, j + N_HALF, jnp.minimum(j, 2 * N_HALF - 1) - N_HALF))

    def wout_map(i, j):
        return (jnp.where(j <= N_HALF, N_HALF - 1, j - N_HALF - 1), 0)

    return pl.pallas_call(
        _gmlp_kernel,
        grid=(t // TM, 2 * N_HALF + 1),
        in_specs=[
            pl.BlockSpec((TM, D_MODEL), lambda i, j: (i, 0)),
            pl.BlockSpec((None, 1, D_MODEL), lambda i, j: (layer, 0, 0)),
            pl.BlockSpec((D_MODEL, TG), win_map),
            pl.BlockSpec((N_HALF, 1, TG), lambda i, j: (0, 0, 0)),
            pl.BlockSpec((N_HALF, 1, TG), lambda i, j: (0, 0, 0)),
            pl.BlockSpec((None, GMLP_GROUPS, CHUNK, CHUNK), lambda i, j: (mixer, 0, 0, 0)),
            pl.BlockSpec((GMLP_GROUPS, CHUNK, GMLP_GROUP_DIM), lambda i, j: (0, 0, 0)),
            pl.BlockSpec((TG, D_MODEL), wout_map),
        ],
        out_specs=pl.BlockSpec((TM, D_MODEL), lambda i, j: (i, 0)),
        out_shape=jax.ShapeDtypeStruct((t, D_MODEL), F32),
        scratch_shapes=[
            pltpu.VMEM((TM, D_MODEL), BF16),
            pltpu.VMEM((N_HALF, TM, TG), F32),
            pltpu.VMEM((TM, LANES), F32),
            pltpu.VMEM((TM, LANES), F32),
            pltpu.VMEM((TM, 1), F32),
            pltpu.VMEM((TM, 1), F32),
            pltpu.VMEM((TM, TG), BF16),
            pltpu.VMEM((TM, TG), BF16),
        ],
        compiler_params=_cparams(("parallel", "arbitrary")),
        name="gmlp",
    )(x, g, w_in, ln_g, ln_b, w_s, b_s, w_out)


def _rope(y, ca, cp, cm):
    n = y.shape[-1]
    reps = n // LANES
    ca = jnp.concatenate([ca] * reps, axis=-1)
    cp = jnp.concatenate([cp] * reps, axis=-1)
    cm = jnp.concatenate([cm] * reps, axis=-1)
    half = ROPE_DIM // 2
    return y * ca + pltpu.roll(y, n - half, 1) * cp + pltpu.roll(y, half, 1) * cm


def _dup_heads(y):
    lane = lax.broadcasted_iota(jnp.int32, (y.shape[0], LANES), 1)
    low = lane < HEAD_DIM
    out = []
    for p in range(y.shape[-1] // LANES):
        blk = y[:, p * LANES:(p + 1) * LANES]
        swapped = pltpu.roll(blk, HEAD_DIM, 1)
        out.append(jnp.where(low, blk, swapped))
        out.append(jnp.where(low, swapped, blk))
    return jnp.concatenate(out, axis=-1)


def _qkv_kernel(x_ref, g_ref, w_ref, b_ref, ca_ref, cp_ref, cm_ref, q_ref, k_ref, v_ref):
    h = _rms(x_ref[...], g_ref[...]).astype(BF16)
    ca, cp, cm = ca_ref[...], cp_ref[...], cm_ref[...]
    for n in range(QW // TN):
        cols = slice(n * TN, (n + 1) * TN)
        y = _dot(h, w_ref[:, cols]) + b_ref[:, cols]
        q_ref[:, cols] = (_rope(y, ca, cp, cm) * (HEAD_DIM ** -0.5)).astype(BF16)
    y = _dot(h, w_ref[:, QW:]) + b_ref[:, QW:]
    k_ref[...] = _dup_heads(_rope(y[:, :KW], ca, cp, cm)).astype(BF16)
    v_ref[...] = _dup_heads(y[:, KW:]).astype(BF16)


def _qkv(x, g, w, b, ca, cp, cm, seq, layer, mixer):
    t = x.shape[0]
    wide = QW + 2 * KW
    pos_blocks = seq // TM
    tab = pl.BlockSpec((TM, LANES), lambda i: (i % pos_blocks, 0))
    kv_shape = jax.ShapeDtypeStruct((t, N_KV_HEADS * LANES), BF16)
    return pl.pallas_call(
        _qkv_kernel,
        grid=(t // TM,),
        in_specs=[
            pl.BlockSpec((TM, D_MODEL), lambda i: (i, 0)),
            pl.BlockSpec((None, 1, D_MODEL), lambda i: (layer, 0, 0)),
            pl.BlockSpec((D_MODEL, wide), lambda i: (0, 0)),
            pl.BlockSpec((None, 1, wide), lambda i: (mixer, 0, 0)),
            tab, tab, tab,
        ],
        out_specs=[
            pl.BlockSpec((TM, QW), lambda i: (i, 0)),
            pl.BlockSpec((TM, N_KV_HEADS * LANES), lambda i: (i, 0)),
            pl.BlockSpec((TM, N_KV_HEADS * LANES), lambda i: (i, 0)),
        ],
        out_shape=[jax.ShapeDtypeStruct((t, QW), BF16), kv_shape, kv_shape],
        compiler_params=_cparams(("parallel",)),
        name="qkv",
    )(x, g, w, b, ca, cp, cm)


def _attn_kernel(sink_ref, q_ref, kp_ref, kc_ref, vp_ref, vc_ref, o_ref):
    nb = pl.program_id(1)
    t_idx = lax.broadcasted_iota(jnp.int32, (WINDOW, 2 * WINDOW), 0)
    s_idx = lax.broadcasted_iota(jnp.int32, (WINDOW, 2 * WINDOW), 1)
    diff = t_idx + WINDOW - s_idx
    valid = (diff >= 0) & (diff < WINDOW) & ((s_idx >= WINDOW) | (nb > 0))
    lane = lax.broadcasted_iota(jnp.int32, (WINDOW, LANES), 1)
    low = lane < HEAD_DIM
    zero = jnp.zeros((), BF16)

    for kh in range(N_KV_HEADS):
        ks = slice(kh * LANES, (kh + 1) * LANES)
        k2 = jnp.concatenate([kp_ref[:, ks], kc_ref[:, ks]], axis=0)
        v2 = jnp.concatenate([vp_ref[:, ks], vc_ref[:, ks]], axis=0)
        for pp in range(Q_PER_KV // 2):
            p = kh * (Q_PER_KV // 2) + pp
            qb = q_ref[:, p * LANES:(p + 1) * LANES]
            halves = []
            for hh in range(2):
                head = 2 * p + hh
                qm = jnp.where(low if hh == 0 else ~low, qb, zero)
                s = lax.dot_general(qm, k2, (((1,), (1,)), ((), ())),
                                    preferred_element_type=F32)
                s = jnp.where(valid, s, -jnp.inf)
                sink = sink_ref[head]
                m = jnp.maximum(jnp.max(s, axis=-1, keepdims=True), sink)
                e = jnp.exp(s - m)
                denom = jnp.sum(e, axis=-1, keepdims=True) + jnp.exp(sink - m)
                halves.append(_dot(e.astype(BF16), v2) * (1.0 / denom))
            o_ref[:, p * LANES:(p + 1) * LANES] = jnp.where(low, halves[0], halves[1]).astype(BF16)


def _attn(sinks, q, k2, v2, batch, seq):
    t = q.shape[0]
    nblk = seq // WINDOW
    kvw = N_KV_HEADS * LANES
    cur = lambda b, n: (b * nblk + n, 0)
    prev = lambda b, n: (b * nblk + jnp.maximum(n - 1, 0), 0)
    return pl.pallas_call(
        _attn_kernel,
        grid=(batch, nblk),
        in_specs=[
            pl.BlockSpec(memory_space=pltpu.SMEM),
            pl.BlockSpec((WINDOW, QW), cur),
            pl.BlockSpec((WINDOW, kvw), prev),
            pl.BlockSpec((WINDOW, kvw), cur),
            pl.BlockSpec((WINDOW, kvw), prev),
            pl.BlockSpec((WINDOW, kvw), cur),
        ],
        out_specs=pl.BlockSpec((WINDOW, QW), cur),
        out_shape=jax.ShapeDtypeStruct((t, QW), BF16),
        compiler_params=_cparams(("parallel", "arbitrary")),
        name="attn",
    )(sinks, q, k2, k2, v2, v2)


def _wo_kernel(x_ref, a_ref, w_ref, b_ref, o_ref):
    o_ref[...] = x_ref[...] + _dot(a_ref[...], w_ref[...]) + b_ref[...]


def _wo(x, a, w, b, mixer):
    t = x.shape[0]
    return pl.pallas_call(
        _wo_kernel,
        grid=(t // TM,),
        in_specs=[
            pl.BlockSpec((TM, D_MODEL), lambda i: (i, 0)),
            pl.BlockSpec((TM, QW), lambda i: (i, 0)),
            pl.BlockSpec((QW, D_MODEL), lambda i: (0, 0)),
            pl.BlockSpec((None, 1, D_MODEL), lambda i: (mixer, 0, 0)),
        ],
        out_specs=pl.BlockSpec((TM, D_MODEL), lambda i: (i, 0)),
        out_shape=jax.ShapeDtypeStruct((t, D_MODEL), F32),
        compiler_params=_cparams(("parallel",)),
        name="wo",
    )(x, a, w, b)


def _ple_kernel(x_ref, p_ref, g_ref, wg_ref, wp_ref, fg_ref, o_ref, *, final):
    xf = x_ref[...]
    h = _rms(xf, g_ref[...]).astype(BF16)
    gate = jax.nn.sigmoid(_dot(h, wg_ref[...]))
    proj = _dot(p_ref[...].astype(BF16), wp_ref[...])
    y = xf + gate * proj
    if final:
        y = _rms(y, fg_ref[...])
    o_ref[...] = y


def _ple(x, p, g, wg, wp, fg, layer, final):
    t = x.shape[0]
    return pl.pallas_call(
        functools.partial(_ple_kernel, final=final),
        grid=(t // TM,),
        in_specs=[
            pl.BlockSpec((TM, D_MODEL), lambda i: (i, 0)),
            pl.BlockSpec((None, TM, PLE_DIM), lambda i: (layer, i, 0)),
            pl.BlockSpec((None, 1, D_MODEL), lambda i: (layer, 0, 0)),
            pl.BlockSpec((D_MODEL, D_MODEL), lambda i: (0, 0)),
            pl.BlockSpec((PLE_DIM, D_MODEL), lambda i: (0, 0)),
            pl.BlockSpec((1, D_MODEL), lambda i: (0, 0)),
        ],
        out_specs=pl.BlockSpec((TM, D_MODEL), lambda i: (i, 0)),
        out_shape=jax.ShapeDtypeStruct((t, D_MODEL), F32),
        compiler_params=_cparams(("parallel",)),
        name="ple_final" if final else "ple",
    )(x, p, g, wg, wp, fg)


def _rope_coefficients(seq):
    half = ROPE_DIM // 2
    inv_freq = np.float32(ROPE_THETA) ** (-np.arange(0, ROPE_DIM, 2, dtype=np.float32) / np.float32(ROPE_DIM))
    ang = (np.arange(seq, dtype=np.float32)[:, None] * inv_freq[None, :]).astype(np.float32)
    cos = np.cos(ang.astype(np.float64)).astype(np.float32)
    sin = np.sin(ang.astype(np.float64)).astype(np.float32)
    ones = np.ones((seq, HEAD_DIM - ROPE_DIM), np.float32)
    zeros = np.zeros((seq, HEAD_DIM - half), np.float32)
    ca = np.concatenate([cos, cos, ones], axis=-1)
    cp = np.concatenate([-sin, zeros], axis=-1)
    cm = np.concatenate([np.zeros((seq, half), np.float32), sin, zeros[:, half:]], axis=-1)
    tile = lambda a: jnp.asarray(np.concatenate([a] * (LANES // HEAD_DIM), axis=-1))
    return tile(ca), tile(cp), tile(cm)


def kernel(x, p, ffn1_norm, ffn1_w1, ffn1_w3, ffn1_w2, mix_norm, ffn2_norm, ffn2_w1, ffn2_w3, ffn2_w2, ple_norm, ple_w_gate, ple_w_proj, gmlp_w_in, gmlp_ln_g, gmlp_ln_b, gmlp_w_s, gmlp_b_s, gmlp_w_out, swa_wq, swa_bq, swa_wk, swa_bk, swa_wv, swa_bv, swa_sinks, swa_wo, swa_bo, final_norm):
    batch, seq, _ = x.shape
    t = batch * seq
    assert seq % TM == 0 and TM % WINDOW == 0 and t % TM_FFN == 0
    rows = lambda a: a.reshape(a.shape[0], 1, a.shape[1])
    xs = x.reshape(t, D_MODEL)
    ps = p.reshape(DEPTH, t, PLE_DIM)
    ca, cp, cm = _rope_coefficients(seq)

    ffn1_g, ffn2_g, mix_g, ple_g = rows(ffn1_norm), rows(ffn2_norm), rows(mix_norm), rows(ple_norm)
    swa_bqkv = rows(jnp.concatenate([swa_bq, swa_bk, swa_bv], axis=1))
    fg = final_norm.reshape(1, D_MODEL)

    ffn_w = [_cast_bf16(0, w) for w in (ffn1_w1, ffn1_w3, ffn1_w2)]
    for i in range(DEPTH):
        j = i // 2
        if i % 2 == 0:
            mixer_jobs = [_CastJob(j, (gmlp_w_in,)), _CastJob(j, (gmlp_w_out,))]
        else:
            mixer_jobs = [_CastJob(j, (swa_wq, swa_wk, swa_wv)), _CastJob(j, (swa_wo,))]
        ffn2_jobs = [_CastJob(i, (w,)) for w in (ffn2_w1, ffn2_w3, ffn2_w2)]
        xs, (mix_w0, mix_w1, *ffn_w) = _ffn(xs, ffn1_g, *ffn_w, layer=i, jobs=mixer_jobs + ffn2_jobs)
        if i % 2 == 0:
            b_s = jnp.broadcast_to(gmlp_b_s[j][:, :, None], (GMLP_GROUPS, CHUNK, GMLP_GROUP_DIM))
            xs = _gmlp(xs, mix_g, mix_w0,
                       gmlp_ln_g[j].reshape(N_HALF, 1, TG), gmlp_ln_b[j].reshape(N_HALF, 1, TG),
                       gmlp_w_s, b_s, mix_w1, layer=i, mixer=j)
        else:
            q, k2, v2 = _qkv(xs, mix_g, mix_w0, swa_bqkv, ca, cp, cm, seq, layer=i, mixer=j)
            a = _attn(swa_sinks[j], q, k2, v2, batch, seq)
            xs = _wo(xs, a, mix_w1, rows(swa_bo), mixer=j)
        ple_jobs = [_CastJob(i, (ple_w_gate,)), _CastJob(i, (ple_w_proj,))]
        next_jobs = [_CastJob(i + 1, (w,)) for w in (ffn1_w1, ffn1_w3, ffn1_w2)] if i + 1 < DEPTH else []
        xs, (ple_wg, ple_wp, *ffn_w) = _ffn(xs, ffn2_g, *ffn_w, layer=i, jobs=ple_jobs + next_jobs)
        xs = _ple(xs, ps, ple_g, ple_wg, ple_wp, fg, layer=i, final=(i == DEPTH - 1))
    return xs.reshape(batch, seq, D_MODEL)
```

```python
import functools
import math
from typing import NamedTuple

import jax
import jax.numpy as jnp
import numpy as np
from jax import lax
from jax.experimental import pallas as pl
from jax.experimental.pallas import tpu as pltpu

F32 = jnp.float32
BF16 = jnp.bfloat16

D_MODEL = 2048
DEPTH = 2
D_FF = 5632
PLE_DIM = 256
RMS_EPS = 1e-6
LN_EPS = 1e-5
CHUNK = 128
GMLP_WIDTH = 2 * D_MODEL
GMLP_GROUPS = 16
GMLP_GROUP_DIM = GMLP_WIDTH // GMLP_GROUPS
N_Q_HEADS = 32
N_KV_HEADS = 4
HEAD_DIM = 64
Q_PER_KV = N_Q_HEADS // N_KV_HEADS
WINDOW = 128
ROPE_THETA = 500000.0
ROPE_DIM = HEAD_DIM // 4
LOG2_E = 1.4426950408889634
QW = N_Q_HEADS * HEAD_DIM
KW = N_KV_HEADS * HEAD_DIM

LANES = 128
BF16_SUBLANES = 16

TM = 512
TM_FFN = 1024
TF = 512
TN = 512
TG = 1024
N_HALF = GMLP_WIDTH // TG
GROUPS_PER_SLAB = TG // GMLP_GROUP_DIM
CAST_BLOCK_BYTES = 6 * 1024 * 1024
VMEM_LIMIT = 62 * 1024 * 1024


def _cparams(sem):
    return pltpu.CompilerParams(dimension_semantics=sem, vmem_limit_bytes=VMEM_LIMIT)


def _dot(a, b):
    return jnp.dot(a, b, preferred_element_type=F32)


def _rms(xf, g):
    ms = jnp.mean(xf * xf, axis=-1, keepdims=True)
    return xf * lax.rsqrt(ms + RMS_EPS) * g


def _gelu(z):
    return 0.5 * z * (1.0 + lax.erf(z * (1.0 / math.sqrt(2.0))))


def _cast_kernel(*refs):
    *in_refs, o_ref = refs
    col = 0
    for r in in_refs:
        w = r.shape[-1]
        o_ref[:, col:col + w] = r[...].astype(BF16)
        col += w


def _cast_bf16(layer, *ws):
    rows = ws[0].shape[1]
    cols = sum(w.shape[2] for w in ws)
    rb = rows
    while rb * cols * 4 > CAST_BLOCK_BYTES and rb % 32 == 0:
        rb //= 2
    return pl.pallas_call(
        _cast_kernel,
        grid=(rows // rb,),
        in_specs=[pl.BlockSpec((None, rb, w.shape[2]), lambda r: (layer, r, 0)) for w in ws],
        out_specs=pl.BlockSpec((rb, cols), lambda r: (r, 0)),
        out_shape=jax.ShapeDtypeStruct((rows, cols), BF16),
        compiler_params=_cparams(("parallel",)),
        name="cast",
    )(*ws)


class _CastJob(NamedTuple):
    layer: int
    arrays: tuple


def _cast_job_blocking(rows, n_tiles, n_steps):
    per_tile = rows // n_tiles
    for n_sub in range(n_steps, 0, -1):
        if per_tile % n_sub == 0 and (per_tile // n_sub) % BF16_SUBLANES == 0:
            return per_tile // n_sub, n_sub
    raise ValueError(f"cannot spread a {rows}-row cast over {n_tiles} tiles")


def _ffn_kernel(*refs, job_sizes):
    n_cast_in = sum(job_sizes)
    x_ref, g_ref, w1_ref, w3_ref, w2_ref = refs[:5]
    cast_in = refs[5:5 + n_cast_in]
    o_ref = refs[5 + n_cast_in]
    cast_out = refs[6 + n_cast_in:6 + n_cast_in + len(job_sizes)]
    h_ref = refs[-1]
    f = pl.program_id(1)

    def swiglu(h):
        a = _dot(h, w1_ref[...])
        b = _dot(h, w3_ref[...])
        gate = (0.5 * (a * jax.nn.sigmoid(a)) * b).astype(BF16)
        return _dot(gate, w2_ref[...])

    def side_casts():
        start = 0
        for n, out_ref in zip(job_sizes, cast_out):
            _cast_kernel(*cast_in[start:start + n], out_ref)
            start += n

    @pl.when(f == 0)
    def _():
        xf = x_ref[...]
        h = _rms(xf, g_ref[...]).astype(BF16)
        h_ref[...] = h
        o_ref[...] = xf + swiglu(h)
        side_casts()

    @pl.when(f > 0)
    def _():
        o_ref[...] += swiglu(h_ref[...])
        side_casts()


def _ffn(x, g, w1, w3, w2, layer, jobs):
    t = x.shape[0]
    n_tiles, n_steps = t // TM_FFN, D_FF // TF
    in_specs = [
        pl.BlockSpec((TM_FFN, D_MODEL), lambda i, f: (i, 0)),
        pl.BlockSpec((None, 1, D_MODEL), lambda i, f: (layer, 0, 0)),
        pl.BlockSpec((D_MODEL, TF), lambda i, f: (0, f)),
        pl.BlockSpec((D_MODEL, TF), lambda i, f: (0, f)),
        pl.BlockSpec((TF, D_MODEL), lambda i, f: (f, 0)),
    ]
    out_specs = [pl.BlockSpec((TM_FFN, D_MODEL), lambda i, f: (i, 0))]
    out_shape = [jax.ShapeDtypeStruct((t, D_MODEL), F32)]
    cast_inputs = []
    for job in jobs:
        rows = job.arrays[0].shape[1]
        cols = sum(a.shape[2] for a in job.arrays)
        rb, n_sub = _cast_job_blocking(rows, n_tiles, n_steps)
        block = lambda i, f, n_sub=n_sub: i * n_sub + jnp.minimum(f, n_sub - 1)
        for a in job.arrays:
            in_specs.append(pl.BlockSpec(
                (None, rb, a.shape[2]),
                lambda i, f, block=block, l=job.layer: (l, block(i, f), 0)))
            cast_inputs.append(a)
        out_specs.append(pl.BlockSpec((rb, cols), lambda i, f, block=block: (block(i, f), 0)))
        out_shape.append(jax.ShapeDtypeStruct((rows, cols), BF16))
    y, *cast = pl.pallas_call(
        functools.partial(_ffn_kernel, job_sizes=tuple(len(job.arrays) for job in jobs)),
        grid=(n_tiles, n_steps),
        in_specs=in_specs,
        out_specs=out_specs,
        out_shape=out_shape,
        scratch_shapes=[pltpu.VMEM((TM_FFN, D_MODEL), BF16)],
        compiler_params=_cparams(("arbitrary", "arbitrary")),
        name="ffn",
    )(x, g, w1, w3, w2, *cast_inputs)
    return y, cast


def _gmlp_kernel(x_ref, g_ref, win_ref, lng_ref, lnb_ref, ws_ref, bs_ref, wout_ref,
                 o_ref, h_ref, v_ref, s1_ref, s2_ref, mu_ref, rs_ref, gt0_ref, gt1_ref):
    j = pl.program_id(1)
    gt_refs = (gt0_ref, gt1_ref)

    def v_slab(h):
        z = _gelu(_dot(h, win_ref[...]))
        v_ref[j] = z
        p1 = z[:, :LANES]
        p2 = p1 * p1
        for c in range(1, TG // LANES):
            zc = z[:, c * LANES:(c + 1) * LANES]
            p1 += zc
            p2 += zc * zc
        return p1, p2

    @pl.when(j == 0)
    def _():
        h = _rms(x_ref[...], g_ref[...]).astype(BF16)
        h_ref[...] = h
        s1_ref[...], s2_ref[...] = v_slab(h)

    @pl.when((j > 0) & (j < N_HALF))
    def _():
        p1, p2 = v_slab(h_ref[...])
        s1_ref[...] += p1
        s2_ref[...] += p2

    def gate(k, dst_ref):
        u = _gelu(_dot(h_ref[...], win_ref[...]))
        if isinstance(k, int) and k == 0:
            mu = jnp.sum(s1_ref[...], axis=-1, keepdims=True) * (1.0 / GMLP_WIDTH)
            ex2 = jnp.sum(s2_ref[...], axis=-1, keepdims=True) * (1.0 / GMLP_WIDTH)
            rs = lax.rsqrt(ex2 - mu * mu + LN_EPS)
            mu_ref[...] = mu
            rs_ref[...] = rs
        else:
            mu, rs = mu_ref[...], rs_ref[...]
        vn = ((v_ref[k] - mu) * rs * lng_ref[k] + lnb_ref[k]).astype(BF16)
        row = lax.broadcasted_iota(jnp.int32, (CHUNK, CHUNK), 0)
        col = lax.broadcasted_iota(jnp.int32, (CHUNK, CHUNK), 1)
        causal = row >= col
        for gg in range(GROUPS_PER_SLAB):
            grp = GROUPS_PER_SLAB * k + gg
            w = jnp.where(causal, ws_ref[grp], 0.0).astype(BF16)
            bias = bs_ref[grp]
            cols = slice(gg * GMLP_GROUP_DIM, (gg + 1) * GMLP_GROUP_DIM)
            for c in range(TM // CHUNK):
                rows = slice(c * CHUNK, (c + 1) * CHUNK)
                s = _dot(w, vn[rows, cols]) + bias
                dst_ref[rows, cols] = (u[rows, cols] * s).astype(BF16)

    def project(src_ref):
        o_ref[...] += _dot(src_ref[...], wout_ref[...])

    @pl.when(j == N_HALF)
    def _():
        o_ref[...] = x_ref[...]
        gate(0, gt_refs[0])

    for parity in range(2):
        @pl.when((j > N_HALF) & (j < 2 * N_HALF) & (((j - N_HALF) & 1) == parity))
        def _():
            project(gt_refs[1 - parity])
            gate(j - N_HALF, gt_refs[parity])

    @pl.when(j == 2 * N_HALF)
    def _():
        project(gt_refs[(N_HALF - 1) % 2])


def _gmlp(x, g, w_in, ln_g, ln_b, w_s, b_s, w_out, layer, mixer):
    t = x.shape[0]

    def win_map(i, j):
        return (0, jnp.where(j < N_HALF, j + N_HALF, jnp.minimum(j, 2 * N_HALF - 1) - N_HALF))

    def wout_map(i, j):
        return (jnp.where(j <= N_HALF, N_HALF - 1, j - N_HALF - 1), 0)

    return pl.pallas_call(
        _gmlp_kernel,
        grid=(t // TM, 2 * N_HALF + 1),
        in_specs=[
            pl.BlockSpec((TM, D_MODEL), lambda i, j: (i, 0)),
            pl.BlockSpec((None, 1, D_MODEL), lambda i, j: (layer, 0, 0)),
            pl.BlockSpec((D_MODEL, TG), win_map),
            pl.BlockSpec((N_HALF, 1, TG), lambda i, j: (0, 0, 0)),
            pl.BlockSpec((N_HALF, 1, TG), lambda i, j: (0, 0, 0)),
            pl.BlockSpec((None, GMLP_GROUPS, CHUNK, CHUNK), lambda i, j: (mixer, 0, 0, 0)),
            pl.BlockSpec((GMLP_GROUPS, CHUNK, GMLP_GROUP_DIM), lambda i, j: (0, 0, 0)),
            pl.BlockSpec((TG, D_MODEL), wout_map),
        ],
        out_specs=pl.BlockSpec((TM, D_MODEL), lambda i, j: (i, 0)),
        out_shape=jax.ShapeDtypeStruct((t, D_MODEL), F32),
        scratch_shapes=[
            pltpu.VMEM((TM, D_MODEL), BF16),
            pltpu.VMEM((N_HALF, TM, TG), F32),
            pltpu.VMEM((TM, LANES), F32),
            pltpu.VMEM((TM, LANES), F32),
            pltpu.VMEM((TM, 1), F32),
            pltpu.VMEM((TM, 1), F32),
            pltpu.VMEM((TM, TG), BF16),
            pltpu.VMEM((TM, TG), BF16),
        ],
        compiler_params=_cparams(("parallel", "arbitrary")),
        name="gmlp",
    )(x, g, w_in, ln_g, ln_b, w_s, b_s, w_out)


def _rope(y, ca, cp, cm):
    n = y.shape[-1]
    reps = n // LANES
    ca = jnp.concatenate([ca] * reps, axis=-1)
    cp = jnp.concatenate([cp] * reps, axis=-1)
    cm = jnp.concatenate([cm] * reps, axis=-1)
    half = ROPE_DIM // 2
    return y * ca + pltpu.roll(y, n - half, 1) * cp + pltpu.roll(y, half, 1) * cm


def _dup_heads(y):
    lane = lax.broadcasted_iota(jnp.int32, (y.shape[0], LANES), 1)
    low = lane < HEAD_DIM
    out = []
    for p in range(y.shape[-1] // LANES):
        blk = y[:, p * LANES:(p + 1) * LANES]
        swapped = pltpu.roll(blk, HEAD_DIM, 1)
        out.append(jnp.where(low, blk, swapped))
        out.append(jnp.where(low, swapped, blk))
    return jnp.concatenate(out, axis=-1)


def _qkv_kernel(x_ref, g_ref, w_ref, b_ref, ca_ref, cp_ref, cm_ref, q_ref, k_ref, v_ref):
    h = _rms(x_ref[...], g_ref[...]).astype(BF16)
    ca, cp, cm = ca_ref[...], cp_ref[...], cm_ref[...]
    for n in range(QW // TN):
        cols = slice(n * TN, (n + 1) * TN)
        y = _dot(h, w_ref[:, cols]) + b_ref[:, cols]
        q_ref[:, cols] = (_rope(y, ca, cp, cm) * (HEAD_DIM ** -0.5 * LOG2_E)).astype(BF16)
    y = _dot(h, w_ref[:, QW:]) + b_ref[:, QW:]
    k_ref[...] = _dup_heads(_rope(y[:, :KW], ca, cp, cm)).astype(BF16)
    v_ref[...] = _dup_heads(y[:, KW:]).astype(BF16)


def _qkv(x, g, w, b, ca, cp, cm, seq, layer, mixer):
    t = x.shape[0]
    wide = QW + 2 * KW
    pos_blocks = seq // TM
    tab = pl.BlockSpec((TM, LANES), lambda i: (i % pos_blocks, 0))
    kv_shape = jax.ShapeDtypeStruct((t, N_KV_HEADS * LANES), BF16)
    return pl.pallas_call(
        _qkv_kernel,
        grid=(t // TM,),
        in_specs=[
            pl.BlockSpec((TM, D_MODEL), lambda i: (i, 0)),
            pl.BlockSpec((None, 1, D_MODEL), lambda i: (layer, 0, 0)),
            pl.BlockSpec((D_MODEL, wide), lambda i: (0, 0)),
            pl.BlockSpec((None, 1, wide), lambda i: (mixer, 0, 0)),
            tab, tab, tab,
        ],
        out_specs=[
            pl.BlockSpec((TM, QW), lambda i: (i, 0)),
            pl.BlockSpec((TM, N_KV_HEADS * LANES), lambda i: (i, 0)),
            pl.BlockSpec((TM, N_KV_HEADS * LANES), lambda i: (i, 0)),
        ],
        out_shape=[jax.ShapeDtypeStruct((t, QW), BF16), kv_shape, kv_shape],
        compiler_params=_cparams(("parallel",)),
        name="qkv",
    )(x, g, w, b, ca, cp, cm)


def _attn_kernel(sink_ref, q_ref, kp_ref, kc_ref, vp_ref, vc_ref, o_ref):
    nb = pl.program_id(1)
    t_idx = lax.broadcasted_iota(jnp.int32, (WINDOW, 2 * WINDOW), 0)
    s_idx = lax.broadcasted_iota(jnp.int32, (WINDOW, 2 * WINDOW), 1)
    diff = t_idx + WINDOW - s_idx
    valid = (diff >= 0) & (diff < WINDOW) & ((s_idx >= WINDOW) | (nb > 0))
    lane = lax.broadcasted_iota(jnp.int32, (WINDOW, LANES), 1)
    low = lane < HEAD_DIM
    zero = jnp.zeros((), BF16)
    sink_col = lax.broadcasted_iota(jnp.int32, (1, 2 * WINDOW), 1) == 0
    sink_row = lax.broadcasted_iota(jnp.int32, (2 * WINDOW, LANES), 0) == 0

    for kh in range(N_KV_HEADS):
        ks = slice(kh * LANES, (kh + 1) * LANES)
        k2 = jnp.concatenate([kp_ref[:, ks], kc_ref[:, ks]], axis=0)
        v2 = jnp.concatenate([vp_ref[:, ks], vc_ref[:, ks]], axis=0)
        v2 = jnp.where(sink_row, zero, v2)
        for pp in range(Q_PER_KV // 2):
            p = kh * (Q_PER_KV // 2) + pp
            qb = q_ref[:, p * LANES:(p + 1) * LANES]
            halves = []
            for hh in range(2):
                head = 2 * p + hh
                qm = jnp.where(low if hh == 0 else ~low, qb, zero)
                s = lax.dot_general(qm, k2, (((1,), (1,)), ((), ())),
                                    preferred_element_type=F32)
                fill = jnp.where(sink_col, sink_ref[head] * LOG2_E, -jnp.inf)
                s = jnp.where(valid, s, fill)
                e = jnp.exp2(s - jnp.max(s, axis=-1, keepdims=True))
                denom = jnp.sum(e, axis=-1, keepdims=True)
                halves.append(_dot(e.astype(BF16), v2) * (1.0 / denom))
            o_ref[:, p * LANES:(p + 1) * LANES] = jnp.where(low, halves[0], halves[1]).astype(BF16)


def _attn(sinks, q, k2, v2, batch, seq):
    t = q.shape[0]
    nblk = seq // WINDOW
    kvw = N_KV_HEADS * LANES
    cur = lambda b, n: (b * nblk + n, 0)
    prev = lambda b, n: (b * nblk + jnp.maximum(n - 1, 0), 0)
    return pl.pallas_call(
        _attn_kernel,
        grid=(batch, nblk),
        in_specs=[
            pl.BlockSpec(memory_space=pltpu.SMEM),
            pl.BlockSpec((WINDOW, QW), cur),
            pl.BlockSpec((WINDOW, kvw), prev),
            pl.BlockSpec((WINDOW, kvw), cur),
            pl.BlockSpec((WINDOW, kvw), prev),
            pl.BlockSpec((WINDOW, kvw), cur),
        ],
        out_specs=pl.BlockSpec((WINDOW, QW), cur),
        out_shape=jax.ShapeDtypeStruct((t, QW), BF16),
        compiler_params=_cparams(("parallel", "arbitrary")),
        name="attn",
    )(sinks, q, k2, k2, v2, v2)


def _wo_kernel(x_ref, a_ref, w_ref, b_ref, o_ref):
    o_ref[...] = x_ref[...] + _dot(a_ref[...], w_ref[...]) + b_ref[...]


def _wo(x, a, w, b, mixer):
    t = x.shape[0]
    return pl.pallas_call(
        _wo_kernel,
        grid=(t // TM,),
        in_specs=[
            pl.BlockSpec((TM, D_MODEL), lambda i: (i, 0)),
            pl.BlockSpec((TM, QW), lambda i: (i, 0)),
            pl.BlockSpec((QW, D_MODEL), lambda i: (0, 0)),
            pl.BlockSpec((None, 1, D_MODEL), lambda i: (mixer, 0, 0)),
        ],
        out_specs=pl.BlockSpec((TM, D_MODEL), lambda i: (i, 0)),
        out_shape=jax.ShapeDtypeStruct((t, D_MODEL), F32),
        compiler_params=_cparams(("parallel",)),
        name="wo",
    )(x, a, w, b)


def _ple_kernel(x_ref, p_ref, g_ref, wg_ref, wp_ref, fg_ref, o_ref, *, final):
    xf = x_ref[...]
    h = _rms(xf, g_ref[...]).astype(BF16)
    gate = jax.nn.sigmoid(_dot(h, wg_ref[...]))
    proj = _dot(p_ref[...].astype(BF16), wp_ref[...])
    y = xf + gate * proj
    if final:
        y = _rms(y, fg_ref[...])
    o_ref[...] = y


def _ple(x, p, g, wg, wp, fg, layer, final):
    t = x.shape[0]
    return pl.pallas_call(
        functools.partial(_ple_kernel, final=final),
        grid=(t // TM,),
        in_specs=[
            pl.BlockSpec((TM, D_MODEL), lambda i: (i, 0)),
            pl.BlockSpec((None, TM, PLE_DIM), lambda i: (layer, i, 0)),
            pl.BlockSpec((None, 1, D_MODEL), lambda i: (layer, 0, 0)),
            pl.BlockSpec((D_MODEL, D_MODEL), lambda i: (0, 0)),
            pl.BlockSpec((PLE_DIM, D_MODEL), lambda i: (0, 0)),
            pl.BlockSpec((1, D_MODEL), lambda i: (0, 0)),
        ],
        out_specs=pl.BlockSpec((TM, D_MODEL), lambda i: (i, 0)),
        out_shape=jax.ShapeDtypeStruct((t, D_MODEL), F32),
        compiler_params=_cparams(("parallel",)),
        name="ple_final" if final else "ple",
    )(x, p, g, wg, wp, fg)


def _rope_coefficients(seq):
    half = ROPE_DIM // 2
    inv_freq = np.float32(ROPE_THETA) ** (-np.arange(0, ROPE_DIM, 2, dtype=np.float32) / np.float32(ROPE_DIM))
    ang = (np.arange(seq, dtype=np.float32)[:, None] * inv_freq[None, :]).astype(np.float32)
    cos = np.cos(ang.astype(np.float64)).astype(np.float32)
    sin = np.sin(ang.astype(np.float64)).astype(np.float32)
    ones = np.ones((seq, HEAD_DIM - ROPE_DIM), np.float32)
    zeros = np.zeros((seq, HEAD_DIM - half), np.float32)
    ca = np.concatenate([cos, cos, ones], axis=-1)
    cp = np.concatenate([-sin, zeros], axis=-1)
    cm = np.concatenate([np.zeros((seq, half), np.float32), sin, zeros[:, half:]], axis=-1)
    tile = lambda a: jnp.asarray(np.concatenate([a] * (LANES // HEAD_DIM), axis=-1))
    return tile(ca), tile(cp), tile(cm)


def kernel(x, p, ffn1_norm, ffn1_w1, ffn1_w3, ffn1_w2, mix_norm, ffn2_norm, ffn2_w1, ffn2_w3, ffn2_w2, ple_norm, ple_w_gate, ple_w_proj, gmlp_w_in, gmlp_ln_g, gmlp_ln_b, gmlp_w_s, gmlp_b_s, gmlp_w_out, swa_wq, swa_bq, swa_wk, swa_bk, swa_wv, swa_bv, swa_sinks, swa_wo, swa_bo, final_norm):
    batch, seq, _ = x.shape
    t = batch * seq
    assert seq % TM == 0 and TM % WINDOW == 0 and t % TM_FFN == 0
    rows = lambda a: a.reshape(a.shape[0], 1, a.shape[1])
    xs = x.reshape(t, D_MODEL)
    ps = p.reshape(DEPTH, t, PLE_DIM)
    ca, cp, cm = _rope_coefficients(seq)

    ffn1_g, ffn2_g, mix_g, ple_g = rows(ffn1_norm), rows(ffn2_norm), rows(mix_norm), rows(ple_norm)
    swa_bqkv = rows(jnp.concatenate([swa_bq, swa_bk, swa_bv], axis=1))
    fg = final_norm.reshape(1, D_MODEL)

    ffn_w = [_cast_bf16(0, w) for w in (ffn1_w1, ffn1_w3, ffn1_w2)]
    for i in range(DEPTH):
        j = i // 2
        if i % 2 == 0:
            mixer_jobs = [_CastJob(j, (gmlp_w_in,)), _CastJob(j, (gmlp_w_out,))]
        else:
            mixer_jobs = [_CastJob(j, (swa_wq, swa_wk, swa_wv)), _CastJob(j, (swa_wo,))]
        ffn2_jobs = [_CastJob(i, (w,)) for w in (ffn2_w1, ffn2_w3, ffn2_w2)]
        xs, (mix_w0, mix_w1, *ffn_w) = _ffn(xs, ffn1_g, *ffn_w, layer=i, jobs=mixer_jobs + ffn2_jobs)
        if i % 2 == 0:
            b_s = jnp.broadcast_to(gmlp_b_s[j][:, :, None], (GMLP_GROUPS, CHUNK, GMLP_GROUP_DIM))
            xs = _gmlp(xs, mix_g, mix_w0,
                       gmlp_ln_g[j].reshape(N_HALF, 1, TG), gmlp_ln_b[j].reshape(N_HALF, 1, TG),
                       gmlp_w_s, b_s, mix_w1, layer=i, mixer=j)
        else:
            q, k2, v2 = _qkv(xs, mix_g, mix_w0, swa_bqkv, ca, cp, cm, seq, layer=i, mixer=j)
            a = _attn(swa_sinks[j], q, k2, v2, batch, seq)
            xs = _wo(xs, a, mix_w1, rows(swa_bo), mixer=j)
        ple_jobs = [_CastJob(i, (ple_w_gate,)), _CastJob(i, (ple_w_proj,))]
        next_jobs = [_CastJob(i + 1, (w,)) for w in (ffn1_w1, ffn1_w3, ffn1_w2)] if i + 1 < DEPTH else []
        xs, (ple_wg, ple_wp, *ffn_w) = _ffn(xs, ffn2_g, *ffn_w, layer=i, jobs=ple_jobs + next_jobs)
        xs = _ple(xs, ps, ple_g, ple_wg, ple_wp, fg, layer=i, final=(i == DEPTH - 1))
    return xs.reshape(batch, seq, D_MODEL)
```

```python
import functools
import math
from typing import NamedTuple

import jax
import jax.numpy as jnp
import numpy as np
from jax import lax
from jax.experimental import pallas as pl
from jax.experimental.pallas import tpu as pltpu

F32 = jnp.float32
BF16 = jnp.bfloat16

D_MODEL = 2048
DEPTH = 2
D_FF = 5632
PLE_DIM = 256
RMS_EPS = 1e-6
LN_EPS = 1e-5
CHUNK = 128
GMLP_WIDTH = 2 * D_MODEL
GMLP_GROUPS = 16
GMLP_GROUP_DIM = GMLP_WIDTH // GMLP_GROUPS
N_Q_HEADS = 32
N_KV_HEADS = 4
HEAD_DIM = 64
Q_PER_KV = N_Q_HEADS // N_KV_HEADS
WINDOW = 128
ROPE_THETA = 500000.0
ROPE_DIM = HEAD_DIM // 4
LOG2_E = 1.4426950408889634
QW = N_Q_HEADS * HEAD_DIM
KW = N_KV_HEADS * HEAD_DIM

LANES = 128
BF16_SUBLANES = 16

TM = 512
TM_FFN = 1024
TF = 512
TN = 512
TG = 1024
N_HALF = GMLP_WIDTH // TG
GROUPS_PER_SLAB = TG // GMLP_GROUP_DIM
CAST_BLOCK_BYTES = 6 * 1024 * 1024
VMEM_LIMIT = 62 * 1024 * 1024


def _cparams(sem):
    return pltpu.CompilerParams(dimension_semantics=sem, vmem_limit_bytes=VMEM_LIMIT)


def _dot(a, b):
    return jnp.dot(a, b, preferred_element_type=F32)


def _rms(xf, g):
    ms = jnp.mean(xf * xf, axis=-1, keepdims=True)
    return xf * lax.rsqrt(ms + RMS_EPS) * g


def _gelu(z):
    return 0.5 * z * (1.0 + lax.erf(z * (1.0 / math.sqrt(2.0))))


def _cast_kernel(*refs):
    *in_refs, o_ref = refs
    col = 0
    for r in in_refs:
        w = r.shape[-1]
        o_ref[:, col:col + w] = r[...].astype(BF16)
        col += w


def _cast_bf16(layer, *ws):
    rows = ws[0].shape[1]
    cols = sum(w.shape[2] for w in ws)
    rb = rows
    while rb * cols * 4 > CAST_BLOCK_BYTES and rb % 32 == 0:
        rb //= 2
    return pl.pallas_call(
        _cast_kernel,
        grid=(rows // rb,),
        in_specs=[pl.BlockSpec((None, rb, w.shape[2]), lambda r: (layer, r, 0)) for w in ws],
        out_specs=pl.BlockSpec((rb, cols), lambda r: (r, 0)),
        out_shape=jax.ShapeDtypeStruct((rows, cols), BF16),
        compiler_params=_cparams(("parallel",)),
        name="cast",
    )(*ws)


class _CastJob(NamedTuple):
    layer: int
    arrays: tuple


def _cast_job_blocking(rows, n_tiles, n_steps):
    per_tile = rows // n_tiles
    for n_sub in range(n_steps, 0, -1):
        if per_tile % n_sub == 0 and (per_tile // n_sub) % BF16_SUBLANES == 0:
            return per_tile // n_sub, n_sub
    raise ValueError(f"cannot spread a {rows}-row cast over {n_tiles} tiles")


def _ffn_kernel(*refs, job_sizes):
    n_cast_in = sum(job_sizes)
    x_ref, g_ref, w1_ref, w3_ref, w2_ref = refs[:5]
    cast_in = refs[5:5 + n_cast_in]
    o_ref = refs[5 + n_cast_in]
    cast_out = refs[6 + n_cast_in:6 + n_cast_in + len(job_sizes)]
    h_ref = refs[-1]
    f = pl.program_id(1)

    def swiglu(h):
        a = _dot(h, w1_ref[...])
        b = _dot(h, w3_ref[...])
        gate = (0.5 * (a * jax.nn.sigmoid(a)) * b).astype(BF16)
        return _dot(gate, w2_ref[...])

    def side_casts():
        start = 0
        for n, out_ref in zip(job_sizes, cast_out):
            _cast_kernel(*cast_in[start:start + n], out_ref)
            start += n

    @pl.when(f == 0)
    def _():
        xf = x_ref[...]
        h = _rms(xf, g_ref[...]).astype(BF16)
        h_ref[...] = h
        o_ref[...] = xf + swiglu(h)
        side_casts()

    @pl.when(f > 0)
    def _():
        o_ref[...] += swiglu(h_ref[...])
        side_casts()


def _ffn(x, g, w1, w3, w2, layer, jobs):
    t = x.shape[0]
    n_tiles, n_steps = t // TM_FFN, D_FF // TF
    in_specs = [
        pl.BlockSpec((TM_FFN, D_MODEL), lambda i, f: (i, 0)),
        pl.BlockSpec((None, 1, D_MODEL), lambda i, f: (layer, 0, 0)),
        pl.BlockSpec((D_MODEL, TF), lambda i, f: (0, f)),
        pl.BlockSpec((D_MODEL, TF), lambda i, f: (0, f)),
        pl.BlockSpec((TF, D_MODEL), lambda i, f: (f, 0)),
    ]
    out_specs = [pl.BlockSpec((TM_FFN, D_MODEL), lambda i, f: (i, 0))]
    out_shape = [jax.ShapeDtypeStruct((t, D_MODEL), F32)]
    cast_inputs = []
    for job in jobs:
        rows = job.arrays[0].shape[1]
        cols = sum(a.shape[2] for a in job.arrays)
        rb, n_sub = _cast_job_blocking(rows, n_tiles, n_steps)
        block = lambda i, f, n_sub=n_sub: i * n_sub + jnp.minimum(f, n_sub - 1)
        for a in job.arrays:
            in_specs.append(pl.BlockSpec(
                (None, rb, a.shape[2]),
                lambda i, f, block=block, l=job.layer: (l, block(i, f), 0)))
            cast_inputs.append(a)
        out_specs.append(pl.BlockSpec((rb, cols), lambda i, f, block=block: (block(i, f), 0)))
        out_shape.append(jax.ShapeDtypeStruct((rows, cols), BF16))
    y, *cast = pl.pallas_call(
        functools.partial(_ffn_kernel, job_sizes=tuple(len(job.arrays) for job in jobs)),
        grid=(n_tiles, n_steps),
        in_specs=in_specs,
        out_specs=out_specs,
        out_shape=out_shape,
        scratch_shapes=[pltpu.VMEM((TM_FFN, D_MODEL), BF16)],
        compiler_params=_cparams(("arbitrary", "arbitrary")),
        name="ffn",
    )(x, g, w1, w3, w2, *cast_inputs)
    return y, cast


def _gmlp_kernel(x_ref, g_ref, win_ref, lng_ref, lnb_ref, ws_ref, bs_ref, wout_ref,
                 o_ref, h_ref, v_ref, s1_ref, s2_ref, mu_ref, rs_ref, gt0_ref, gt1_ref):
    j = pl.program_id(1)
    gt_refs = (gt0_ref, gt1_ref)

    def v_slab(h):
        z = _gelu(_dot(h, win_ref[...]))
        v_ref[j] = z
        p1 = z[:, :LANES]
        p2 = p1 * p1
        for c in range(1, TG // LANES):
            zc = z[:, c * LANES:(c + 1) * LANES]
            p1 += zc
            p2 += zc * zc
        return p1, p2

    @pl.when(j == 0)
    def _():
        xf = x_ref[...]
        o_ref[...] = xf
        h = _rms(xf, g_ref[...]).astype(BF16)
        h_ref[...] = h
        s1_ref[...], s2_ref[...] = v_slab(h)

    @pl.when((j > 0) & (j < N_HALF))
    def _():
        p1, p2 = v_slab(h_ref[...])
        s1_ref[...] += p1
        s2_ref[...] += p2

    def gate(k, dst_ref):
        u = _gelu(_dot(h_ref[...], win_ref[...]))
        if isinstance(k, int) and k == 0:
            mu = jnp.sum(s1_ref[...], axis=-1, keepdims=True) * (1.0 / GMLP_WIDTH)
            ex2 = jnp.sum(s2_ref[...], axis=-1, keepdims=True) * (1.0 / GMLP_WIDTH)
            rs = lax.rsqrt(ex2 - mu * mu + LN_EPS)
            mu_ref[...] = mu
            rs_ref[...] = rs
        else:
            mu, rs = mu_ref[...], rs_ref[...]
        vn = ((v_ref[k] - mu) * rs * lng_ref[k] + lnb_ref[k]).astype(BF16)
        row = lax.broadcasted_iota(jnp.int32, (CHUNK, CHUNK), 0)
        col = lax.broadcasted_iota(jnp.int32, (CHUNK, CHUNK), 1)
        causal = row >= col
        for gg in range(GROUPS_PER_SLAB):
            grp = GROUPS_PER_SLAB * k + gg
            w = jnp.where(causal, ws_ref[grp], 0.0).astype(BF16)
            bias = bs_ref[grp]
            cols = slice(gg * GMLP_GROUP_DIM, (gg + 1) * GMLP_GROUP_DIM)
            for c in range(TM // CHUNK):
                rows = slice(c * CHUNK, (c + 1) * CHUNK)
                s = _dot(w, vn[rows, cols]) + bias
                dst_ref[rows, cols] = (u[rows, cols] * s).astype(BF16)

    def project(src_ref):
        o_ref[...] += _dot(src_ref[...], wout_ref[...])

    @pl.when(j == N_HALF)
    def _():
        gate(0, gt_refs[0])

    for parity in range(2):
        @pl.when((j > N_HALF) & (j < 2 * N_HALF) & (((j - N_HALF) & 1) == parity))
        def _():
            project(gt_refs[1 - parity])
            gate(j - N_HALF, gt_refs[parity])

    @pl.when(j == 2 * N_HALF)
    def _():
        project(gt_refs[(N_HALF - 1) % 2])


def _gmlp(x, g, w_in, ln_g, ln_b, w_s, b_s, w_out, layer, mixer):
    t = x.shape[0]

    def win_map(i, j):
        return (0, jnp.where(j < N_HALF, j + N_HALF, jnp.minimum(j, 2 * N_HALF - 1) - N_HALF))

    def wout_map(i, j):
        return (jnp.where(j <= N_HALF, N_HALF - 1, j - N_HALF - 1), 0)

    def x_map(i, j):
        return (jnp.minimum(i + (j >= 2 * N_HALF - 1).astype(jnp.int32), t // TM - 1), 0)

    return pl.pallas_call(
        _gmlp_kernel,
        grid=(t // TM, 2 * N_HALF + 1),
        in_specs=[
            pl.BlockSpec((TM, D_MODEL), x_map),
            pl.BlockSpec((None, 1, D_MODEL), lambda i, j: (layer, 0, 0)),
            pl.BlockSpec((D_MODEL, TG), win_map),
            pl.BlockSpec((N_HALF, 1, TG), lambda i, j: (0, 0, 0)),
            pl.BlockSpec((N_HALF, 1, TG), lambda i, j: (0, 0, 0)),
            pl.BlockSpec((None, GMLP_GROUPS, CHUNK, CHUNK), lambda i, j: (mixer, 0, 0, 0)),
            pl.BlockSpec((GMLP_GROUPS, CHUNK, GMLP_GROUP_DIM), lambda i, j: (0, 0, 0)),
            pl.BlockSpec((TG, D_MODEL), wout_map),
        ],
        out_specs=pl.BlockSpec((TM, D_MODEL), lambda i, j: (i, 0)),
        out_shape=jax.ShapeDtypeStruct((t, D_MODEL), F32),
        scratch_shapes=[
            pltpu.VMEM((TM, D_MODEL), BF16),
            pltpu.VMEM((N_HALF, TM, TG), F32),
            pltpu.VMEM((TM, LANES), F32),
            pltpu.VMEM((TM, LANES), F32),
            pltpu.VMEM((TM, 1), F32),
            pltpu.VMEM((TM, 1), F32),
            pltpu.VMEM((TM, TG), BF16),
            pltpu.VMEM((TM, TG), BF16),
        ],
        compiler_params=_cparams(("parallel", "arbitrary")),
        name="gmlp",
    )(x, g, w_in, ln_g, ln_b, w_s, b_s, w_out)


def _rope(y, ca, cp, cm):
    n = y.shape[-1]
    reps = n // LANES
    ca = jnp.concatenate([ca] * reps, axis=-1)
    cp = jnp.concatenate([cp] * reps, axis=-1)
    cm = jnp.concatenate([cm] * reps, axis=-1)
    half = ROPE_DIM // 2
    return y * ca + pltpu.roll(y, n - half, 1) * cp + pltpu.roll(y, half, 1) * cm


def _dup_heads(y):
    lane = lax.broadcasted_iota(jnp.int32, (y.shape[0], LANES), 1)
    low = lane < HEAD_DIM
    out = []
    for p in range(y.shape[-1] // LANES):
        blk = y[:, p * LANES:(p + 1) * LANES]
        swapped = pltpu.roll(blk, HEAD_DIM, 1)
        out.append(jnp.where(low, blk, swapped))
        out.append(jnp.where(low, swapped, blk))
    return jnp.concatenate(out, axis=-1)


def _qkv_kernel(x_ref, g_ref, w_ref, b_ref, ca_ref, cp_ref, cm_ref, q_ref, k_ref, v_ref):
    h = _rms(x_ref[...], g_ref[...]).astype(BF16)
    ca, cp, cm = ca_ref[...], cp_ref[...], cm_ref[...]
    for n in range(QW // TN):
        cols = slice(n * TN, (n + 1) * TN)
        y = _dot(h, w_ref[:, cols]) + b_ref[:, cols]
        q_ref[:, cols] = (_rope(y, ca, cp, cm) * (HEAD_DIM ** -0.5 * LOG2_E)).astype(BF16)
    y = _dot(h, w_ref[:, QW:]) + b_ref[:, QW:]
    k_ref[...] = _dup_heads(_rope(y[:, :KW], ca, cp, cm)).astype(BF16)
    v_ref[...] = _dup_heads(y[:, KW:]).astype(BF16)


def _qkv(x, g, w, b, ca, cp, cm, seq, layer, mixer):
    t = x.shape[0]
    wide = QW + 2 * KW
    pos_blocks = seq // TM
    tab = pl.BlockSpec((TM, LANES), lambda i: (i % pos_blocks, 0))
    kv_shape = jax.ShapeDtypeStruct((t, N_KV_HEADS * LANES), BF16)
    return pl.pallas_call(
        _qkv_kernel,
        grid=(t // TM,),
        in_specs=[
            pl.BlockSpec((TM, D_MODEL), lambda i: (i, 0)),
            pl.BlockSpec((None, 1, D_MODEL), lambda i: (layer, 0, 0)),
            pl.BlockSpec((D_MODEL, wide), lambda i: (0, 0)),
            pl.BlockSpec((None, 1, wide), lambda i: (mixer, 0, 0)),
            tab, tab, tab,
        ],
        out_specs=[
            pl.BlockSpec((TM, QW), lambda i: (i, 0)),
            pl.BlockSpec((TM, N_KV_HEADS * LANES), lambda i: (i, 0)),
            pl.BlockSpec((TM, N_KV_HEADS * LANES), lambda i: (i, 0)),
        ],
        out_shape=[jax.ShapeDtypeStruct((t, QW), BF16), kv_shape, kv_shape],
        compiler_params=_cparams(("parallel",)),
        name="qkv",
    )(x, g, w, b, ca, cp, cm)


def _attn_kernel(sink_ref, q_ref, kp_ref, kc_ref, vp_ref, vc_ref, o_ref, kbuf_ref, vbuf_ref, *,
                 tiles_per_seq):
    first_tile = (pl.program_id(0) % tiles_per_seq) == 0
    kbuf_ref[:WINDOW] = kp_ref[...]
    kbuf_ref[WINDOW:] = kc_ref[...]
    vbuf_ref[:WINDOW] = vp_ref[...]
    vbuf_ref[WINDOW:] = vc_ref[...]

    t_idx = lax.broadcasted_iota(jnp.int32, (WINDOW, 2 * WINDOW), 0)
    s_idx = lax.broadcasted_iota(jnp.int32, (WINDOW, 2 * WINDOW), 1)
    diff = t_idx + WINDOW - s_idx
    band = (diff >= 0) & (diff < WINDOW)
    lane = lax.broadcasted_iota(jnp.int32, (WINDOW, LANES), 1)
    low = lane < HEAD_DIM
    zero = jnp.zeros((), BF16)
    sink_col = lax.broadcasted_iota(jnp.int32, (1, 2 * WINDOW), 1) == 0
    sink_row = lax.broadcasted_iota(jnp.int32, (2 * WINDOW, LANES), 0) == 0

    def block(r, carry):
        row0 = pl.multiple_of(r * WINDOW, WINDOW)
        q_rows = pl.ds(row0, WINDOW)
        kv_rows = pl.ds(row0, 2 * WINDOW)
        has_prev = (r > 0) | jnp.logical_not(first_tile)
        valid = band & ((s_idx >= WINDOW) | has_prev)
        for kh in range(N_KV_HEADS):
            ks = slice(kh * LANES, (kh + 1) * LANES)
            k2 = kbuf_ref[kv_rows, ks]
            v2 = jnp.where(sink_row, zero, vbuf_ref[kv_rows, ks])
            for pp in range(Q_PER_KV // 2):
                p = kh * (Q_PER_KV // 2) + pp
                cols = slice(p * LANES, (p + 1) * LANES)
                qb = q_ref[q_rows, cols]
                halves = []
                for hh in range(2):
                    head = 2 * p + hh
                    qm = jnp.where(low if hh == 0 else ~low, qb, zero)
                    s = lax.dot_general(qm, k2, (((1,), (1,)), ((), ())),
                                        preferred_element_type=F32)
                    fill = jnp.where(sink_col, sink_ref[head] * LOG2_E, -jnp.inf)
                    s = jnp.where(valid, s, fill)
                    e = jnp.exp2(s - jnp.max(s, axis=-1, keepdims=True))
                    denom = jnp.sum(e, axis=-1, keepdims=True)
                    halves.append(_dot(e.astype(BF16), v2) * (1.0 / denom))
                o_ref[q_rows, cols] = jnp.where(low, halves[0], halves[1]).astype(BF16)
        return carry

    lax.fori_loop(0, TM // WINDOW, block, 0)


def _attn(sinks, q, k2, v2, seq):
    t = q.shape[0]
    kvw = N_KV_HEADS * LANES
    blocks_per_tile = TM // WINDOW
    cur = lambda i: (i, 0)
    prev = lambda i: (jnp.maximum(i * blocks_per_tile - 1, 0), 0)
    return pl.pallas_call(
        functools.partial(_attn_kernel, tiles_per_seq=seq // TM),
        grid=(t // TM,),
        in_specs=[
            pl.BlockSpec(memory_space=pltpu.SMEM),
            pl.BlockSpec((TM, QW), cur),
            pl.BlockSpec((WINDOW, kvw), prev),
            pl.BlockSpec((TM, kvw), cur),
            pl.BlockSpec((WINDOW, kvw), prev),
            pl.BlockSpec((TM, kvw), cur),
        ],
        out_specs=pl.BlockSpec((TM, QW), cur),
        out_shape=jax.ShapeDtypeStruct((t, QW), BF16),
        scratch_shapes=[pltpu.VMEM((WINDOW + TM, kvw), BF16), pltpu.VMEM((WINDOW + TM, kvw), BF16)],
        compiler_params=_cparams(("parallel",)),
        name="attn",
    )(sinks, q, k2, k2, v2, v2)


def _wo_kernel(x_ref, a_ref, w_ref, b_ref, o_ref):
    o_ref[...] = x_ref[...] + _dot(a_ref[...], w_ref[...]) + b_ref[...]


def _wo(x, a, w, b, mixer):
    t = x.shape[0]
    return pl.pallas_call(
        _wo_kernel,
        grid=(t // TM,),
        in_specs=[
            pl.BlockSpec((TM, D_MODEL), lambda i: (i, 0)),
            pl.BlockSpec((TM, QW), lambda i: (i, 0)),
            pl.BlockSpec((QW, D_MODEL), lambda i: (0, 0)),
            pl.BlockSpec((None, 1, D_MODEL), lambda i: (mixer, 0, 0)),
        ],
        out_specs=pl.BlockSpec((TM, D_MODEL), lambda i: (i, 0)),
        out_shape=jax.ShapeDtypeStruct((t, D_MODEL), F32),
        compiler_params=_cparams(("parallel",)),
        name="wo",
    )(x, a, w, b)


def _ple_kernel(x_ref, p_ref, g_ref, wg_ref, wp_ref, fg_ref, o_ref, *, final):
    xf = x_ref[...]
    h = _rms(xf, g_ref[...]).astype(BF16)
    gate = jax.nn.sigmoid(_dot(h, wg_ref[...]))
    proj = _dot(p_ref[...].astype(BF16), wp_ref[...])
    y = xf + gate * proj
    if final:
        y = _rms(y, fg_ref[...])
    o_ref[...] = y


def _ple(x, p, g, wg, wp, fg, layer, final):
    t = x.shape[0]
    return pl.pallas_call(
        functools.partial(_ple_kernel, final=final),
        grid=(t // TM,),
        in_specs=[
            pl.BlockSpec((TM, D_MODEL), lambda i: (i, 0)),
            pl.BlockSpec((None, TM, PLE_DIM), lambda i: (layer, i, 0)),
            pl.BlockSpec((None, 1, D_MODEL), lambda i: (layer, 0, 0)),
            pl.BlockSpec((D_MODEL, D_MODEL), lambda i: (0, 0)),
            pl.BlockSpec((PLE_DIM, D_MODEL), lambda i: (0, 0)),
            pl.BlockSpec((1, D_MODEL), lambda i: (0, 0)),
        ],
        out_specs=pl.BlockSpec((TM, D_MODEL), lambda i: (i, 0)),
        out_shape=jax.ShapeDtypeStruct((t, D_MODEL), F32),
        compiler_params=_cparams(("parallel",)),
        name="ple_final" if final else "ple",
    )(x, p, g, wg, wp, fg)


def _rope_coefficients(seq):
    half = ROPE_DIM // 2
    inv_freq = np.float32(ROPE_THETA) ** (-np.arange(0, ROPE_DIM, 2, dtype=np.float32) / np.float32(ROPE_DIM))
    ang = (np.arange(seq, dtype=np.float32)[:, None] * inv_freq[None, :]).astype(np.float32)
    cos = np.cos(ang.astype(np.float64)).astype(np.float32)
    sin = np.sin(ang.astype(np.float64)).astype(np.float32)
    ones = np.ones((seq, HEAD_DIM - ROPE_DIM), np.float32)
    zeros = np.zeros((seq, HEAD_DIM - half), np.float32)
    ca = np.concatenate([cos, cos, ones], axis=-1)
    cp = np.concatenate([-sin, zeros], axis=-1)
    cm = np.concatenate([np.zeros((seq, half), np.float32), sin, zeros[:, half:]], axis=-1)
    tile = lambda a: jnp.asarray(np.concatenate([a] * (LANES // HEAD_DIM), axis=-1))
    return tile(ca), tile(cp), tile(cm)


def kernel(x, p, ffn1_norm, ffn1_w1, ffn1_w3, ffn1_w2, mix_norm, ffn2_norm, ffn2_w1, ffn2_w3, ffn2_w2, ple_norm, ple_w_gate, ple_w_proj, gmlp_w_in, gmlp_ln_g, gmlp_ln_b, gmlp_w_s, gmlp_b_s, gmlp_w_out, swa_wq, swa_bq, swa_wk, swa_bk, swa_wv, swa_bv, swa_sinks, swa_wo, swa_bo, final_norm):
    batch, seq, _ = x.shape
    t = batch * seq
    assert seq % TM == 0 and TM % WINDOW == 0 and t % TM_FFN == 0
    rows = lambda a: a.reshape(a.shape[0], 1, a.shape[1])
    xs = x.reshape(t, D_MODEL)
    ps = p.reshape(DEPTH, t, PLE_DIM)
    ca, cp, cm = _rope_coefficients(seq)

    ffn1_g, ffn2_g, mix_g, ple_g = rows(ffn1_norm), rows(ffn2_norm), rows(mix_norm), rows(ple_norm)
    swa_bqkv = rows(jnp.concatenate([swa_bq, swa_bk, swa_bv], axis=1))
    fg = final_norm.reshape(1, D_MODEL)

    ffn_w = [_cast_bf16(0, w) for w in (ffn1_w1, ffn1_w3, ffn1_w2)]
    for i in range(DEPTH):
        j = i // 2
        if i % 2 == 0:
            mixer_jobs = [_CastJob(j, (gmlp_w_in,)), _CastJob(j, (gmlp_w_out,))]
        else:
            mixer_jobs = [_CastJob(j, (swa_wq, swa_wk, swa_wv)), _CastJob(j, (swa_wo,))]
        ffn2_jobs = [_CastJob(i, (w,)) for w in (ffn2_w1, ffn2_w3, ffn2_w2)]
        xs, (mix_w0, mix_w1, *ffn_w) = _ffn(xs, ffn1_g, *ffn_w, layer=i, jobs=mixer_jobs + ffn2_jobs)
        if i % 2 == 0:
            b_s = jnp.broadcast_to(gmlp_b_s[j][:, :, None], (GMLP_GROUPS, CHUNK, GMLP_GROUP_DIM))
            xs = _gmlp(xs, mix_g, mix_w0,
                       gmlp_ln_g[j].reshape(N_HALF, 1, TG), gmlp_ln_b[j].reshape(N_HALF, 1, TG),
                       gmlp_w_s, b_s, mix_w1, layer=i, mixer=j)
        else:
            q, k2, v2 = _qkv(xs, mix_g, mix_w0, swa_bqkv, ca, cp, cm, seq, layer=i, mixer=j)
            a = _attn(swa_sinks[j], q, k2, v2, seq)
            xs = _wo(xs, a, mix_w1, rows(swa_bo), mixer=j)
        ple_jobs = [_CastJob(i, (ple_w_gate,)), _CastJob(i, (ple_w_proj,))]
        next_jobs = [_CastJob(i + 1, (w,)) for w in (ffn1_w1, ffn1_w3, ffn1_w2)] if i + 1 < DEPTH else []
        xs, (ple_wg, ple_wp, *ffn_w) = _ffn(xs, ffn2_g, *ffn_w, layer=i, jobs=ple_jobs + next_jobs)
        xs = _ple(xs, ps, ple_g, ple_wg, ple_wp, fg, layer=i, final=(i == DEPTH - 1))
    return xs.reshape(batch, seq, D_MODEL)
```

```python
import functools
import math
from typing import NamedTuple

import jax
import jax.numpy as jnp
import numpy as np
from jax import lax
from jax.experimental import pallas as pl
from jax.experimental.pallas import tpu as pltpu

F32 = jnp.float32
BF16 = jnp.bfloat16

D_MODEL = 2048
DEPTH = 2
D_FF = 5632
PLE_DIM = 256
RMS_EPS = 1e-6
LN_EPS = 1e-5
CHUNK = 128
GMLP_WIDTH = 2 * D_MODEL
GMLP_GROUPS = 16
GMLP_GROUP_DIM = GMLP_WIDTH // GMLP_GROUPS
N_Q_HEADS = 32
N_KV_HEADS = 4
HEAD_DIM = 64
Q_PER_KV = N_Q_HEADS // N_KV_HEADS
WINDOW = 128
ROPE_THETA = 500000.0
ROPE_DIM = HEAD_DIM // 4
LOG2_E = 1.4426950408889634
QW = N_Q_HEADS * HEAD_DIM
KW = N_KV_HEADS * HEAD_DIM

LANES = 128
BF16_SUBLANES = 16

TM = 512
TM_FFN = 1024
TF = 512
TN = 512
TG = 1024
N_HALF = GMLP_WIDTH // TG
GROUPS_PER_SLAB = TG // GMLP_GROUP_DIM
CAST_BLOCK_BYTES = 6 * 1024 * 1024
VMEM_LIMIT = 62 * 1024 * 1024


def _cparams(sem):
    return pltpu.CompilerParams(dimension_semantics=sem, vmem_limit_bytes=VMEM_LIMIT)


def _dot(a, b):
    return jnp.dot(a, b, preferred_element_type=F32)


def _rms(xf, g):
    ms = jnp.mean(xf * xf, axis=-1, keepdims=True)
    return xf * lax.rsqrt(ms + RMS_EPS) * g


def _rms_split(xf, g):
    ms = jnp.mean(xf * xf, axis=-1, keepdims=True)
    return (xf * g).astype(BF16), lax.rsqrt(ms + RMS_EPS)


def _gelu(z):
    return 0.5 * z * (1.0 + lax.erf(z * (1.0 / math.sqrt(2.0))))


def _cast_kernel(*refs):
    *in_refs, o_ref = refs
    col = 0
    for r in in_refs:
        w = r.shape[-1]
        o_ref[:, col:col + w] = r[...].astype(BF16)
        col += w


def _cast_bf16(layer, *ws):
    rows = ws[0].shape[1]
    cols = sum(w.shape[2] for w in ws)
    rb = rows
    while rb * cols * 4 > CAST_BLOCK_BYTES and rb % 32 == 0:
        rb //= 2
    return pl.pallas_call(
        _cast_kernel,
        grid=(rows // rb,),
        in_specs=[pl.BlockSpec((None, rb, w.shape[2]), lambda r: (layer, r, 0)) for w in ws],
        out_specs=pl.BlockSpec((rb, cols), lambda r: (r, 0)),
        out_shape=jax.ShapeDtypeStruct((rows, cols), BF16),
        compiler_params=_cparams(("parallel",)),
        name="cast",
    )(*ws)


class _CastJob(NamedTuple):
    layer: int
    arrays: tuple


def _cast_job_blocking(rows, n_tiles, n_steps):
    per_tile = rows // n_tiles
    for n_sub in range(n_steps, 0, -1):
        if per_tile % n_sub == 0 and (per_tile // n_sub) % BF16_SUBLANES == 0:
            return per_tile // n_sub, n_sub
    raise ValueError(f"cannot spread a {rows}-row cast over {n_tiles} tiles")


def _ffn_kernel(*refs, job_sizes):
    n_cast_in = sum(job_sizes)
    x_ref, g_ref, w1_ref, w3_ref, w2_ref = refs[:5]
    cast_in = refs[5:5 + n_cast_in]
    o_ref = refs[5 + n_cast_in]
    cast_out = refs[6 + n_cast_in:6 + n_cast_in + len(job_sizes)]
    h_ref, rs_ref = refs[-2:]
    f = pl.program_id(1)

    def swiglu(hg, rstd):
        a = _dot(hg, w1_ref[...]) * rstd
        b = _dot(hg, w3_ref[...]) * rstd
        gate = (0.5 * (a * jax.nn.sigmoid(a)) * b).astype(BF16)
        return _dot(gate, w2_ref[...])

    def side_casts():
        start = 0
        for n, out_ref in zip(job_sizes, cast_out):
            _cast_kernel(*cast_in[start:start + n], out_ref)
            start += n

    @pl.when(f == 0)
    def _():
        xf = x_ref[...]
        hg, rstd = _rms_split(xf, g_ref[...])
        h_ref[...] = hg
        rs_ref[...] = rstd
        o_ref[...] = xf + swiglu(hg, rstd)
        side_casts()

    @pl.when(f > 0)
    def _():
        o_ref[...] += swiglu(h_ref[...], rs_ref[...])
        side_casts()


def _ffn(x, g, w1, w3, w2, layer, jobs):
    t = x.shape[0]
    n_tiles, n_steps = t // TM_FFN, D_FF // TF
    in_specs = [
        pl.BlockSpec((TM_FFN, D_MODEL), lambda i, f: (i, 0)),
        pl.BlockSpec((None, 1, D_MODEL), lambda i, f: (layer, 0, 0)),
        pl.BlockSpec((D_MODEL, TF), lambda i, f: (0, f)),
        pl.BlockSpec((D_MODEL, TF), lambda i, f: (0, f)),
        pl.BlockSpec((TF, D_MODEL), lambda i, f: (f, 0)),
    ]
    out_specs = [pl.BlockSpec((TM_FFN, D_MODEL), lambda i, f: (i, 0))]
    out_shape = [jax.ShapeDtypeStruct((t, D_MODEL), F32)]
    cast_inputs = []
    for job in jobs:
        rows = job.arrays[0].shape[1]
        cols = sum(a.shape[2] for a in job.arrays)
        rb, n_sub = _cast_job_blocking(rows, n_tiles, n_steps)
        block = lambda i, f, n_sub=n_sub: i * n_sub + jnp.minimum(f, n_sub - 1)
        for a in job.arrays:
            in_specs.append(pl.BlockSpec(
                (None, rb, a.shape[2]),
                lambda i, f, block=block, l=job.layer: (l, block(i, f), 0)))
            cast_inputs.append(a)
        out_specs.append(pl.BlockSpec((rb, cols), lambda i, f, block=block: (block(i, f), 0)))
        out_shape.append(jax.ShapeDtypeStruct((rows, cols), BF16))
    y, *cast = pl.pallas_call(
        functools.partial(_ffn_kernel, job_sizes=tuple(len(job.arrays) for job in jobs)),
        grid=(n_tiles, n_steps),
        in_specs=in_specs,
        out_specs=out_specs,
        out_shape=out_shape,
        scratch_shapes=[pltpu.VMEM((TM_FFN, D_MODEL), BF16), pltpu.VMEM((TM_FFN, 1), F32)],
        compiler_params=_cparams(("arbitrary", "arbitrary")),
        name="ffn",
    )(x, g, w1, w3, w2, *cast_inputs)
    return y, cast


def _gmlp_kernel(x_ref, g_ref, win_ref, lng_ref, lnb_ref, ws_ref, bs_ref, wout_ref,
                 o_ref, h_ref, v_ref, s1_ref, s2_ref, mu_ref, rs_ref, gt0_ref, gt1_ref):
    j = pl.program_id(1)
    gt_refs = (gt0_ref, gt1_ref)

    def v_slab(h):
        z = _gelu(_dot(h, win_ref[...]))
        v_ref[j] = z
        p1 = z[:, :LANES]
        p2 = p1 * p1
        for c in range(1, TG // LANES):
            zc = z[:, c * LANES:(c + 1) * LANES]
            p1 += zc
            p2 += zc * zc
        return p1, p2

    @pl.when(j == 0)
    def _():
        xf = x_ref[...]
        o_ref[...] = xf
        h = _rms(xf, g_ref[...]).astype(BF16)
        h_ref[...] = h
        s1_ref[...], s2_ref[...] = v_slab(h)

    @pl.when((j > 0) & (j < N_HALF))
    def _():
        p1, p2 = v_slab(h_ref[...])
        s1_ref[...] += p1
        s2_ref[...] += p2

    def gate(k, dst_ref):
        u = _gelu(_dot(h_ref[...], win_ref[...]))
        if isinstance(k, int) and k == 0:
            mu = jnp.sum(s1_ref[...], axis=-1, keepdims=True) * (1.0 / GMLP_WIDTH)
            ex2 = jnp.sum(s2_ref[...], axis=-1, keepdims=True) * (1.0 / GMLP_WIDTH)
            rs = lax.rsqrt(ex2 - mu * mu + LN_EPS)
            mu_ref[...] = mu
            rs_ref[...] = rs
        else:
            mu, rs = mu_ref[...], rs_ref[...]
        vn = ((v_ref[k] - mu) * rs * lng_ref[k] + lnb_ref[k]).astype(BF16)
        row = lax.broadcasted_iota(jnp.int32, (CHUNK, CHUNK), 0)
        col = lax.broadcasted_iota(jnp.int32, (CHUNK, CHUNK), 1)
        causal = row >= col
        for gg in range(GROUPS_PER_SLAB):
            grp = GROUPS_PER_SLAB * k + gg
            w = jnp.where(causal, ws_ref[grp], 0.0).astype(BF16)
            bias = bs_ref[grp]
            cols = slice(gg * GMLP_GROUP_DIM, (gg + 1) * GMLP_GROUP_DIM)
            for c in range(TM // CHUNK):
                rows = slice(c * CHUNK, (c + 1) * CHUNK)
                s = _dot(w, vn[rows, cols]) + bias
                dst_ref[rows, cols] = (u[rows, cols] * s).astype(BF16)

    def project(src_ref):
        o_ref[...] += _dot(src_ref[...], wout_ref[...])

    @pl.when(j == N_HALF)
    def _():
        gate(0, gt_refs[0])

    for parity in range(2):
        @pl.when((j > N_HALF) & (j < 2 * N_HALF) & (((j - N_HALF) & 1) == parity))
        def _():
            project(gt_refs[1 - parity])
            gate(j - N_HALF, gt_refs[parity])

    @pl.when(j == 2 * N_HALF)
    def _():
        project(gt_refs[(N_HALF - 1) % 2])


def _gmlp(x, g, w_in, ln_g, ln_b, w_s, b_s, w_out, layer, mixer):
    t = x.shape[0]

    def win_map(i, j):
        return (0, jnp.where(j < N_HALF, j + N_HALF, jnp.minimum(j, 2 * N_HALF - 1) - N_HALF))

    def wout_map(i, j):
        return (jnp.where(j <= N_HALF, N_HALF - 1, j - N_HALF - 1), 0)

    def x_map(i, j):
        return (jnp.minimum(i + (j >= 2 * N_HALF - 1).astype(jnp.int32), t // TM - 1), 0)

    return pl.pallas_call(
        _gmlp_kernel,
        grid=(t // TM, 2 * N_HALF + 1),
        in_specs=[
            pl.BlockSpec((TM, D_MODEL), x_map),
            pl.BlockSpec((None, 1, D_MODEL), lambda i, j: (layer, 0, 0)),
            pl.BlockSpec((D_MODEL, TG), win_map),
            pl.BlockSpec((N_HALF, 1, TG), lambda i, j: (0, 0, 0)),
            pl.BlockSpec((N_HALF, 1, TG), lambda i, j: (0, 0, 0)),
            pl.BlockSpec((None, GMLP_GROUPS, CHUNK, CHUNK), lambda i, j: (mixer, 0, 0, 0)),
            pl.BlockSpec((GMLP_GROUPS, CHUNK, GMLP_GROUP_DIM), lambda i, j: (0, 0, 0)),
            pl.BlockSpec((TG, D_MODEL), wout_map),
        ],
        out_specs=pl.BlockSpec((TM, D_MODEL), lambda i, j: (i, 0)),
        out_shape=jax.ShapeDtypeStruct((t, D_MODEL), F32),
        scratch_shapes=[
            pltpu.VMEM((TM, D_MODEL), BF16),
            pltpu.VMEM((N_HALF, TM, TG), F32),
            pltpu.VMEM((TM, LANES), F32),
            pltpu.VMEM((TM, LANES), F32),
            pltpu.VMEM((TM, 1), F32),
            pltpu.VMEM((TM, 1), F32),
            pltpu.VMEM((TM, TG), BF16),
            pltpu.VMEM((TM, TG), BF16),
        ],
        compiler_params=_cparams(("parallel", "arbitrary")),
        name="gmlp",
    )(x, g, w_in, ln_g, ln_b, w_s, b_s, w_out)


def _rope(y, ca, cp, cm):
    n = y.shape[-1]
    reps = n // LANES
    ca = jnp.concatenate([ca] * reps, axis=-1)
    cp = jnp.concatenate([cp] * reps, axis=-1)
    cm = jnp.concatenate([cm] * reps, axis=-1)
    half = ROPE_DIM // 2
    return y * ca + pltpu.roll(y, n - half, 1) * cp + pltpu.roll(y, half, 1) * cm


def _dup_heads(y):
    lane = lax.broadcasted_iota(jnp.int32, (y.shape[0], LANES), 1)
    low = lane < HEAD_DIM
    out = []
    for p in range(y.shape[-1] // LANES):
        blk = y[:, p * LANES:(p + 1) * LANES]
        swapped = pltpu.roll(blk, HEAD_DIM, 1)
        out.append(jnp.where(low, blk, swapped))
        out.append(jnp.where(low, swapped, blk))
    return jnp.concatenate(out, axis=-1)


def _qkv_kernel(x_ref, g_ref, w_ref, b_ref, ca_ref, cp_ref, cm_ref, q_ref, k_ref, v_ref):
    hg, rstd = _rms_split(x_ref[...], g_ref[...])
    ca, cp, cm = ca_ref[...], cp_ref[...], cm_ref[...]
    for n in range(QW // TN):
        cols = slice(n * TN, (n + 1) * TN)
        y = _dot(hg, w_ref[:, cols]) * rstd + b_ref[:, cols]
        q_ref[:, cols] = (_rope(y, ca, cp, cm) * (HEAD_DIM ** -0.5 * LOG2_E)).astype(BF16)
    y = _dot(hg, w_ref[:, QW:]) * rstd + b_ref[:, QW:]
    k_ref[...] = _dup_heads(_rope(y[:, :KW], ca, cp, cm)).astype(BF16)
    v_ref[...] = _dup_heads(y[:, KW:]).astype(BF16)


def _qkv(x, g, w, b, ca, cp, cm, seq, layer, mixer):
    t = x.shape[0]
    wide = QW + 2 * KW
    pos_blocks = seq // TM
    tab = pl.BlockSpec((TM, LANES), lambda i: (i % pos_blocks, 0))
    kv_shape = jax.ShapeDtypeStruct((t, N_KV_HEADS * LANES), BF16)
    return pl.pallas_call(
        _qkv_kernel,
        grid=(t // TM,),
        in_specs=[
            pl.BlockSpec((TM, D_MODEL), lambda i: (i, 0)),
            pl.BlockSpec((None, 1, D_MODEL), lambda i: (layer, 0, 0)),
            pl.BlockSpec((D_MODEL, wide), lambda i: (0, 0)),
            pl.BlockSpec((None, 1, wide), lambda i: (mixer, 0, 0)),
            tab, tab, tab,
        ],
        out_specs=[
            pl.BlockSpec((TM, QW), lambda i: (i, 0)),
            pl.BlockSpec((TM, N_KV_HEADS * LANES), lambda i: (i, 0)),
            pl.BlockSpec((TM, N_KV_HEADS * LANES), lambda i: (i, 0)),
        ],
        out_shape=[jax.ShapeDtypeStruct((t, QW), BF16), kv_shape, kv_shape],
        compiler_params=_cparams(("parallel",)),
        name="qkv",
    )(x, g, w, b, ca, cp, cm)


def _attn_kernel(sink_ref, q_ref, kp_ref, kc_ref, vp_ref, vc_ref, o_ref, kbuf_ref, vbuf_ref, *,
                 tiles_per_seq):
    first_tile = (pl.program_id(0) % tiles_per_seq) == 0
    kbuf_ref[:WINDOW] = kp_ref[...]
    kbuf_ref[WINDOW:] = kc_ref[...]
    vbuf_ref[:WINDOW] = vp_ref[...]
    vbuf_ref[WINDOW:] = vc_ref[...]

    t_idx = lax.broadcasted_iota(jnp.int32, (WINDOW, 2 * WINDOW), 0)
    s_idx = lax.broadcasted_iota(jnp.int32, (WINDOW, 2 * WINDOW), 1)
    diff = t_idx + WINDOW - s_idx
    band = (diff >= 0) & (diff < WINDOW)
    lane = lax.broadcasted_iota(jnp.int32, (WINDOW, LANES), 1)
    low = lane < HEAD_DIM
    zero = jnp.zeros((), BF16)
    sink_col = lax.broadcasted_iota(jnp.int32, (1, 2 * WINDOW), 1) == 0
    sink_row = lax.broadcasted_iota(jnp.int32, (2 * WINDOW, LANES), 0) == 0

    def block(r, carry):
        row0 = pl.multiple_of(r * WINDOW, WINDOW)
        q_rows = pl.ds(row0, WINDOW)
        kv_rows = pl.ds(row0, 2 * WINDOW)
        has_prev = (r > 0) | jnp.logical_not(first_tile)
        valid = band & ((s_idx >= WINDOW) | has_prev)
        for kh in range(N_KV_HEADS):
            ks = slice(kh * LANES, (kh + 1) * LANES)
            k2 = kbuf_ref[kv_rows, ks]
            v2 = jnp.where(sink_row, zero, vbuf_ref[kv_rows, ks])
            for pp in range(Q_PER_KV // 2):
                p = kh * (Q_PER_KV // 2) + pp
                cols = slice(p * LANES, (p + 1) * LANES)
                qb = q_ref[q_rows, cols]
                halves = []
                for hh in range(2):
                    head = 2 * p + hh
                    qm = jnp.where(low if hh == 0 else ~low, qb, zero)
                    s = lax.dot_general(qm, k2, (((1,), (1,)), ((), ())),
                                        preferred_element_type=F32)
                    fill = jnp.where(sink_col, sink_ref[head] * LOG2_E, -jnp.inf)
                    s = jnp.where(valid, s, fill)
                    e = jnp.exp2(s - jnp.max(s, axis=-1, keepdims=True))
                    denom = jnp.sum(e, axis=-1, keepdims=True)
                    halves.append(_dot(e.astype(BF16), v2) * (1.0 / denom))
                o_ref[q_rows, cols] = jnp.where(low, halves[0], halves[1]).astype(BF16)
        return carry

    lax.fori_loop(0, TM // WINDOW, block, 0)


def _attn(sinks, q, k2, v2, seq):
    t = q.shape[0]
    kvw = N_KV_HEADS * LANES
    blocks_per_tile = TM // WINDOW
    cur = lambda i: (i, 0)
    prev = lambda i: (jnp.maximum(i * blocks_per_tile - 1, 0), 0)
    return pl.pallas_call(
        functools.partial(_attn_kernel, tiles_per_seq=seq // TM),
        grid=(t // TM,),
        in_specs=[
            pl.BlockSpec(memory_space=pltpu.SMEM),
            pl.BlockSpec((TM, QW), cur),
            pl.BlockSpec((WINDOW, kvw), prev),
            pl.BlockSpec((TM, kvw), cur),
            pl.BlockSpec((WINDOW, kvw), prev),
            pl.BlockSpec((TM, kvw), cur),
        ],
        out_specs=pl.BlockSpec((TM, QW), cur),
        out_shape=jax.ShapeDtypeStruct((t, QW), BF16),
        scratch_shapes=[pltpu.VMEM((WINDOW + TM, kvw), BF16), pltpu.VMEM((WINDOW + TM, kvw), BF16)],
        compiler_params=_cparams(("parallel",)),
        name="attn",
    )(sinks, q, k2, k2, v2, v2)


def _wo_kernel(x_ref, a_ref, w_ref, b_ref, o_ref):
    o_ref[...] = x_ref[...] + _dot(a_ref[...], w_ref[...]) + b_ref[...]


def _wo(x, a, w, b, mixer):
    t = x.shape[0]
    return pl.pallas_call(
        _wo_kernel,
        grid=(t // TM,),
        in_specs=[
            pl.BlockSpec((TM, D_MODEL), lambda i: (i, 0)),
            pl.BlockSpec((TM, QW), lambda i: (i, 0)),
            pl.BlockSpec((QW, D_MODEL), lambda i: (0, 0)),
            pl.BlockSpec((None, 1, D_MODEL), lambda i: (mixer, 0, 0)),
        ],
        out_specs=pl.BlockSpec((TM, D_MODEL), lambda i: (i, 0)),
        out_shape=jax.ShapeDtypeStruct((t, D_MODEL), F32),
        compiler_params=_cparams(("parallel",)),
        name="wo",
    )(x, a, w, b)


def _ple_kernel(x_ref, p_ref, g_ref, wg_ref, wp_ref, fg_ref, o_ref, *, final):
    xf = x_ref[...]
    hg, rstd = _rms_split(xf, g_ref[...])
    gate = jax.nn.sigmoid(_dot(hg, wg_ref[...]) * rstd)
    proj = _dot(p_ref[...].astype(BF16), wp_ref[...])
    y = xf + gate * proj
    if final:
        y = _rms(y, fg_ref[...])
    o_ref[...] = y


def _ple(x, p, g, wg, wp, fg, layer, final):
    t = x.shape[0]
    return pl.pallas_call(
        functools.partial(_ple_kernel, final=final),
        grid=(t // TM,),
        in_specs=[
            pl.BlockSpec((TM, D_MODEL), lambda i: (i, 0)),
            pl.BlockSpec((None, TM, PLE_DIM), lambda i: (layer, i, 0)),
            pl.BlockSpec((None, 1, D_MODEL), lambda i: (layer, 0, 0)),
            pl.BlockSpec((D_MODEL, D_MODEL), lambda i: (0, 0)),
            pl.BlockSpec((PLE_DIM, D_MODEL), lambda i: (0, 0)),
            pl.BlockSpec((1, D_MODEL), lambda i: (0, 0)),
        ],
        out_specs=pl.BlockSpec((TM, D_MODEL), lambda i: (i, 0)),
        out_shape=jax.ShapeDtypeStruct((t, D_MODEL), F32),
        compiler_params=_cparams(("parallel",)),
        name="ple_final" if final else "ple",
    )(x, p, g, wg, wp, fg)


def _rope_coefficients(seq):
    half = ROPE_DIM // 2
    inv_freq = np.float32(ROPE_THETA) ** (-np.arange(0, ROPE_DIM, 2, dtype=np.float32) / np.float32(ROPE_DIM))
    ang = (np.arange(seq, dtype=np.float32)[:, None] * inv_freq[None, :]).astype(np.float32)
    cos = np.cos(ang.astype(np.float64)).astype(np.float32)
    sin = np.sin(ang.astype(np.float64)).astype(np.float32)
    ones = np.ones((seq, HEAD_DIM - ROPE_DIM), np.float32)
    zeros = np.zeros((seq, HEAD_DIM - half), np.float32)
    ca = np.concatenate([cos, cos, ones], axis=-1)
    cp = np.concatenate([-sin, zeros], axis=-1)
    cm = np.concatenate([np.zeros((seq, half), np.float32), sin, zeros[:, half:]], axis=-1)
    tile = lambda a: jnp.asarray(np.concatenate([a] * (LANES // HEAD_DIM), axis=-1))
    return tile(ca), tile(cp), tile(cm)


def kernel(x, p, ffn1_norm, ffn1_w1, ffn1_w3, ffn1_w2, mix_norm, ffn2_norm, ffn2_w1, ffn2_w3, ffn2_w2, ple_norm, ple_w_gate, ple_w_proj, gmlp_w_in, gmlp_ln_g, gmlp_ln_b, gmlp_w_s, gmlp_b_s, gmlp_w_out, swa_wq, swa_bq, swa_wk, swa_bk, swa_wv, swa_bv, swa_sinks, swa_wo, swa_bo, final_norm):
    batch, seq, _ = x.shape
    t = batch * seq
    assert seq % TM == 0 and TM % WINDOW == 0 and t % TM_FFN == 0
    rows = lambda a: a.reshape(a.shape[0], 1, a.shape[1])
    xs = x.reshape(t, D_MODEL)
    ps = p.reshape(DEPTH, t, PLE_DIM)
    ca, cp, cm = _rope_coefficients(seq)

    ffn1_g, ffn2_g, mix_g, ple_g = rows(ffn1_norm), rows(ffn2_norm), rows(mix_norm), rows(ple_norm)
    swa_bqkv = rows(jnp.concatenate([swa_bq, swa_bk, swa_bv], axis=1))
    fg = final_norm.reshape(1, D_MODEL)

    ffn_w = [_cast_bf16(0, w) for w in (ffn1_w1, ffn1_w3, ffn1_w2)]
    for i in range(DEPTH):
        j = i // 2
        if i % 2 == 0:
            mixer_jobs = [_CastJob(j, (gmlp_w_in,)), _CastJob(j, (gmlp_w_out,))]
        else:
            mixer_jobs = [_CastJob(j, (swa_wq, swa_wk, swa_wv)), _CastJob(j, (swa_wo,))]
        ffn2_jobs = [_CastJob(i, (w,)) for w in (ffn2_w1, ffn2_w3, ffn2_w2)]
        xs, (mix_w0, mix_w1, *ffn_w) = _ffn(xs, ffn1_g, *ffn_w, layer=i, jobs=mixer_jobs + ffn2_jobs)
        if i % 2 == 0:
            b_s = jnp.broadcast_to(gmlp_b_s[j][:, :, None], (GMLP_GROUPS, CHUNK, GMLP_GROUP_DIM))
            xs = _gmlp(xs, mix_g, mix_w0,
                       gmlp_ln_g[j].reshape(N_HALF, 1, TG), gmlp_ln_b[j].reshape(N_HALF, 1, TG),
                       gmlp_w_s, b_s, mix_w1, layer=i, mixer=j)
        else:
            q, k2, v2 = _qkv(xs, mix_g, mix_w0, swa_bqkv, ca, cp, cm, seq, layer=i, mixer=j)
            a = _attn(swa_sinks[j], q, k2, v2, seq)
            xs = _wo(xs, a, mix_w1, rows(swa_bo), mixer=j)
        ple_jobs = [_CastJob(i, (ple_w_gate,)), _CastJob(i, (ple_w_proj,))]
        next_jobs = [_CastJob(i + 1, (w,)) for w in (ffn1_w1, ffn1_w3, ffn1_w2)] if i + 1 < DEPTH else []
        xs, (ple_wg, ple_wp, *ffn_w) = _ffn(xs, ffn2_g, *ffn_w, layer=i, jobs=ple_jobs + next_jobs)
        xs = _ple(xs, ps, ple_g, ple_wg, ple_wp, fg, layer=i, final=(i == DEPTH - 1))
    return xs.reshape(batch, seq, D_MODEL)
```

```python
import functools
import math
from typing import NamedTuple

import jax
import jax.numpy as jnp
import numpy as np
from jax import lax
from jax.experimental import pallas as pl
from jax.experimental.pallas import tpu as pltpu

F32 = jnp.float32
BF16 = jnp.bfloat16

D_MODEL = 2048
DEPTH = 2
D_FF = 5632
PLE_DIM = 256
RMS_EPS = 1e-6
LN_EPS = 1e-5
CHUNK = 128
GMLP_WIDTH = 2 * D_MODEL
GMLP_GROUPS = 16
GMLP_GROUP_DIM = GMLP_WIDTH // GMLP_GROUPS
N_Q_HEADS = 32
N_KV_HEADS = 4
HEAD_DIM = 64
Q_PER_KV = N_Q_HEADS // N_KV_HEADS
WINDOW = 128
ROPE_THETA = 500000.0
ROPE_DIM = HEAD_DIM // 4
LOG2_E = 1.4426950408889634
QW = N_Q_HEADS * HEAD_DIM
KW = N_KV_HEADS * HEAD_DIM

LANES = 128
BF16_SUBLANES = 16

TM = 512
TM_FFN = 1024
TF = 512
TF_HEAD = 256
N_CAST_TILES = 8
TN = 512
TG = 1024
N_HALF = GMLP_WIDTH // TG
GROUPS_PER_SLAB = TG // GMLP_GROUP_DIM
VMEM_LIMIT = 62 * 1024 * 1024


def _cparams(sem):
    return pltpu.CompilerParams(dimension_semantics=sem, vmem_limit_bytes=VMEM_LIMIT)


def _dot(a, b):
    return jnp.dot(a, b, preferred_element_type=F32)


def _rms(xf, g):
    ms = jnp.mean(xf * xf, axis=-1, keepdims=True)
    return xf * lax.rsqrt(ms + RMS_EPS) * g


def _rms_split(xf, g):
    ms = jnp.mean(xf * xf, axis=-1, keepdims=True)
    return (xf * g).astype(BF16), lax.rsqrt(ms + RMS_EPS)


def _gelu(z):
    return 0.5 * z * (1.0 + lax.erf(z * (1.0 / math.sqrt(2.0))))


def _cast_block(*refs):
    *in_refs, o_ref = refs
    col = 0
    for r in in_refs:
        w = r.shape[-1]
        o_ref[:, col:col + w] = r[...].astype(BF16)
        col += w


class _CastJob(NamedTuple):
    layer: int
    arrays: tuple


def _cast_job_blocking(rows, n_tiles, n_steps):
    per_tile = rows // n_tiles
    for n_sub in range(n_steps, 0, -1):
        if per_tile % n_sub == 0 and (per_tile // n_sub) % BF16_SUBLANES == 0:
            return per_tile // n_sub, n_sub
    raise ValueError(f"cannot spread a {rows}-row cast over {n_tiles} tiles")


def _ffn_step(f, x_ref, g_ref, weights, o_ref, h_ref, rs_ref, tail):
    def swiglu(hg, rstd, w):
        a = _dot(hg, w[0]) * rstd
        b = _dot(hg, w[1]) * rstd
        gate = (0.5 * (a * jax.nn.sigmoid(a)) * b).astype(BF16)
        return _dot(gate, w[2])

    @pl.when(f == 0)
    def _():
        w = weights()
        xf = x_ref[...]
        hg, rstd = _rms_split(xf, g_ref[...])
        h_ref[...] = hg
        rs_ref[...] = rstd
        o_ref[...] = xf + swiglu(hg, rstd, w)
        tail(w)

    @pl.when(f > 0)
    def _():
        w = weights()
        o_ref[...] += swiglu(h_ref[...], rs_ref[...], w)
        tail(w)


def _ffn_head_kernel(x_ref, g_ref, w1_ref, w3_ref, w2_ref,
                     o_ref, w1b_ref, w3b_ref, w2b_ref, h_ref, rs_ref):
    def weights():
        return tuple(r[...].astype(BF16) for r in (w1_ref, w3_ref, w2_ref))

    def store_bf16(w):
        w1b_ref[...], w3b_ref[...], w2b_ref[...] = w

    _ffn_step(pl.program_id(0), x_ref, g_ref, weights, o_ref, h_ref, rs_ref, store_bf16)


def _ffn_body_kernel(*refs, job_sizes):
    n_cast_in = sum(job_sizes)
    x_ref, g_ref, w1_ref, w3_ref, w2_ref = refs[:5]
    cast_in = refs[5:5 + n_cast_in]
    o_ref = refs[6 + n_cast_in]
    cast_out = refs[7 + n_cast_in:7 + n_cast_in + len(job_sizes)]
    h_ref, rs_ref = refs[-2:]

    def weights():
        return w1_ref[...], w3_ref[...], w2_ref[...]

    def side_casts(_):
        start = 0
        for n, out_ref in zip(job_sizes, cast_out):
            _cast_block(*cast_in[start:start + n], out_ref)
            start += n

    _ffn_step(pl.program_id(1), x_ref, g_ref, weights, o_ref, h_ref, rs_ref, side_casts)


def _ffn(x, g, w1, w3, w2, layer, jobs):
    t = x.shape[0]
    n_tiles, n_steps, n_head_steps = t // TM_FFN, D_FF // TF, D_FF // TF_HEAD
    scratch = [pltpu.VMEM((TM_FFN, D_MODEL), BF16), pltpu.VMEM((TM_FFN, 1), F32)]
    y_shape = jax.ShapeDtypeStruct((t, D_MODEL), F32)
    wide = jax.ShapeDtypeStruct((D_MODEL, D_FF), BF16)
    tall = jax.ShapeDtypeStruct((D_FF, D_MODEL), BF16)

    y_head, w1b, w3b, w2b = pl.pallas_call(
        _ffn_head_kernel,
        grid=(n_head_steps,),
        in_specs=[
            pl.BlockSpec((TM_FFN, D_MODEL), lambda f: (0, 0), pipeline_mode=pl.Buffered(1)),
            pl.BlockSpec((None, 1, D_MODEL), lambda f: (layer, 0, 0)),
            pl.BlockSpec((None, D_MODEL, TF_HEAD), lambda f: (layer, 0, f)),
            pl.BlockSpec((None, D_MODEL, TF_HEAD), lambda f: (layer, 0, f)),
            pl.BlockSpec((None, TF_HEAD, D_MODEL), lambda f: (layer, f, 0)),
        ],
        out_specs=[
            pl.BlockSpec((TM_FFN, D_MODEL), lambda f: (0, 0)),
            pl.BlockSpec((D_MODEL, TF_HEAD), lambda f: (0, f)),
            pl.BlockSpec((D_MODEL, TF_HEAD), lambda f: (0, f)),
            pl.BlockSpec((TF_HEAD, D_MODEL), lambda f: (f, 0)),
        ],
        out_shape=[y_shape, wide, wide, tall],
        scratch_shapes=scratch,
        compiler_params=_cparams(("arbitrary",)),
        name="ffn_head",
    )(x, g, w1, w3, w2)

    in_specs = [
        pl.BlockSpec((TM_FFN, D_MODEL), lambda i, f: (i + 1, 0)),
        pl.BlockSpec((None, 1, D_MODEL), lambda i, f: (layer, 0, 0)),
        pl.BlockSpec((D_MODEL, TF), lambda i, f: (0, f)),
        pl.BlockSpec((D_MODEL, TF), lambda i, f: (0, f)),
        pl.BlockSpec((TF, D_MODEL), lambda i, f: (f, 0)),
    ]
    out_specs = [pl.BlockSpec((TM_FFN, D_MODEL), lambda i, f: (i + 1, 0))]
    out_shape = [y_shape]
    cast_inputs = []
    for job in jobs:
        rows = job.arrays[0].shape[1]
        cols = sum(a.shape[2] for a in job.arrays)
        rb, n_sub = _cast_job_blocking(rows, N_CAST_TILES, n_steps)

        def block(i, f, n_sub=n_sub):
            moving = i * n_sub + jnp.minimum(f, n_sub - 1)
            return jnp.where(i < N_CAST_TILES, moving, N_CAST_TILES * n_sub - 1)

        for a in job.arrays:
            in_specs.append(pl.BlockSpec(
                (None, rb, a.shape[2]),
                lambda i, f, block=block, l=job.layer: (l, block(i, f), 0)))
            cast_inputs.append(a)
        out_specs.append(pl.BlockSpec((rb, cols), lambda i, f, block=block: (block(i, f), 0)))
        out_shape.append(jax.ShapeDtypeStruct((rows, cols), BF16))
    in_specs.append(pl.BlockSpec(memory_space=pl.ANY))
    y, *cast = pl.pallas_call(
        functools.partial(_ffn_body_kernel, job_sizes=tuple(len(job.arrays) for job in jobs)),
        grid=(n_tiles - 1, n_steps),
        in_specs=in_specs,
        out_specs=out_specs,
        out_shape=out_shape,
        input_output_aliases={len(in_specs) - 1: 0},
        scratch_shapes=scratch,
        compiler_params=_cparams(("arbitrary", "arbitrary")),
        name="ffn",
    )(x, g, w1b, w3b, w2b, *cast_inputs, y_head)
    return y, cast


def _gmlp_kernel(x_ref, g_ref, win_ref, lng_ref, lnb_ref, ws_ref, bs_ref, wout_ref,
                 o_ref, h_ref, v_ref, s1_ref, s2_ref, mu_ref, rs_ref, gt0_ref, gt1_ref):
    j = pl.program_id(1)
    gt_refs = (gt0_ref, gt1_ref)

    def v_slab(h):
        z = _gelu(_dot(h, win_ref[...]))
        v_ref[j] = z
        p1 = z[:, :LANES]
        p2 = p1 * p1
        for c in range(1, TG // LANES):
            zc = z[:, c * LANES:(c + 1) * LANES]
            p1 += zc
            p2 += zc * zc
        return p1, p2

    @pl.when(j == 0)
    def _():
        xf = x_ref[...]
        o_ref[...] = xf
        h = _rms(xf, g_ref[...]).astype(BF16)
        h_ref[...] = h
        s1_ref[...], s2_ref[...] = v_slab(h)

    @pl.when((j > 0) & (j < N_HALF))
    def _():
        p1, p2 = v_slab(h_ref[...])
        s1_ref[...] += p1
        s2_ref[...] += p2

    def gate(k, dst_ref):
        u = _gelu(_dot(h_ref[...], win_ref[...]))
        if isinstance(k, int) and k == 0:
            mu = jnp.sum(s1_ref[...], axis=-1, keepdims=True) * (1.0 / GMLP_WIDTH)
            ex2 = jnp.sum(s2_ref[...], axis=-1, keepdims=True) * (1.0 / GMLP_WIDTH)
            rs = lax.rsqrt(ex2 - mu * mu + LN_EPS)
            mu_ref[...] = mu
            rs_ref[...] = rs
        else:
            mu, rs = mu_ref[...], rs_ref[...]
        vn = ((v_ref[k] - mu) * rs * lng_ref[k] + lnb_ref[k]).astype(BF16)
        row = lax.broadcasted_iota(jnp.int32, (CHUNK, CHUNK), 0)
        col = lax.broadcasted_iota(jnp.int32, (CHUNK, CHUNK), 1)
        causal = row >= col
        for gg in range(GROUPS_PER_SLAB):
            grp = GROUPS_PER_SLAB * k + gg
            w = jnp.where(causal, ws_ref[grp], 0.0).astype(BF16)
            bias = bs_ref[grp]
            cols = slice(gg * GMLP_GROUP_DIM, (gg + 1) * GMLP_GROUP_DIM)
            for c in range(TM // CHUNK):
                rows = slice(c * CHUNK, (c + 1) * CHUNK)
                s = _dot(w, vn[rows, cols]) + bias
                dst_ref[rows, cols] = (u[rows, cols] * s).astype(BF16)

    def project(src_ref):
        o_ref[...] += _dot(src_ref[...], wout_ref[...])

    @pl.when(j == N_HALF)
    def _():
        gate(0, gt_refs[0])

    for parity in range(2):
        @pl.when((j > N_HALF) & (j < 2 * N_HALF) & (((j - N_HALF) & 1) == parity))
        def _():
            project(gt_refs[1 - parity])
            gate(j - N_HALF, gt_refs[parity])

    @pl.when(j == 2 * N_HALF)
    def _():
        project(gt_refs[(N_HALF - 1) % 2])


def _gmlp(x, g, w_in, ln_g, ln_b, w_s, b_s, w_out, layer, mixer):
    t = x.shape[0]

    def win_map(i, j):
        return (0, jnp.where(j < N_HALF, j + N_HALF, jnp.minimum(j, 2 * N_HALF - 1) - N_HALF))

    def wout_map(i, j):
        return (jnp.where(j <= N_HALF, N_HALF - 1, j - N_HALF - 1), 0)

    def x_map(i, j):
        return (jnp.minimum(i + (j >= 2 * N_HALF - 1).astype(jnp.int32), t // TM - 1), 0)

    return pl.pallas_call(
        _gmlp_kernel,
        grid=(t // TM, 2 * N_HALF + 1),
        in_specs=[
            pl.BlockSpec((TM, D_MODEL), x_map),
            pl.BlockSpec((None, 1, D_MODEL), lambda i, j: (layer, 0, 0)),
            pl.BlockSpec((D_MODEL, TG), win_map),
            pl.BlockSpec((N_HALF, 1, TG), lambda i, j: (0, 0, 0)),
            pl.BlockSpec((N_HALF, 1, TG), lambda i, j: (0, 0, 0)),
            pl.BlockSpec((None, GMLP_GROUPS, CHUNK, CHUNK), lambda i, j: (mixer, 0, 0, 0)),
            pl.BlockSpec((GMLP_GROUPS, CHUNK, GMLP_GROUP_DIM), lambda i, j: (0, 0, 0)),
            pl.BlockSpec((TG, D_MODEL), wout_map),
        ],
        out_specs=pl.BlockSpec((TM, D_MODEL), lambda i, j: (i, 0)),
        out_shape=jax.ShapeDtypeStruct((t, D_MODEL), F32),
        scratch_shapes=[
            pltpu.VMEM((TM, D_MODEL), BF16),
            pltpu.VMEM((N_HALF, TM, TG), F32),
            pltpu.VMEM((TM, LANES), F32),
            pltpu.VMEM((TM, LANES), F32),
            pltpu.VMEM((TM, 1), F32),
            pltpu.VMEM((TM, 1), F32),
            pltpu.VMEM((TM, TG), BF16),
            pltpu.VMEM((TM, TG), BF16),
        ],
        compiler_params=_cparams(("parallel", "arbitrary")),
        name="gmlp",
    )(x, g, w_in, ln_g, ln_b, w_s, b_s, w_out)


def _rope(y, ca, cp, cm):
    n = y.shape[-1]
    reps = n // LANES
    ca = jnp.concatenate([ca] * reps, axis=-1)
    cp = jnp.concatenate([cp] * reps, axis=-1)
    cm = jnp.concatenate([cm] * reps, axis=-1)
    half = ROPE_DIM // 2
    return y * ca + pltpu.roll(y, n - half, 1) * cp + pltpu.roll(y, half, 1) * cm


def _dup_heads(y):
    lane = lax.broadcasted_iota(jnp.int32, (y.shape[0], LANES), 1)
    low = lane < HEAD_DIM
    out = []
    for p in range(y.shape[-1] // LANES):
        blk = y[:, p * LANES:(p + 1) * LANES]
        swapped = pltpu.roll(blk, HEAD_DIM, 1)
        out.append(jnp.where(low, blk, swapped))
        out.append(jnp.where(low, swapped, blk))
    return jnp.concatenate(out, axis=-1)


def _qkv_kernel(x_ref, g_ref, w_ref, b_ref, ca_ref, cp_ref, cm_ref, q_ref, k_ref, v_ref):
    hg, rstd = _rms_split(x_ref[...], g_ref[...])
    ca, cp, cm = ca_ref[...], cp_ref[...], cm_ref[...]
    for n in range(QW // TN):
        cols = slice(n * TN, (n + 1) * TN)
        y = _dot(hg, w_ref[:, cols]) * rstd + b_ref[:, cols]
        q_ref[:, cols] = (_rope(y, ca, cp, cm) * (HEAD_DIM ** -0.5 * LOG2_E)).astype(BF16)
    y = _dot(hg, w_ref[:, QW:]) * rstd + b_ref[:, QW:]
    k_ref[...] = _dup_heads(_rope(y[:, :KW], ca, cp, cm)).astype(BF16)
    v_ref[...] = _dup_heads(y[:, KW:]).astype(BF16)


def _qkv(x, g, w, b, ca, cp, cm, seq, layer, mixer):
    t = x.shape[0]
    wide = QW + 2 * KW
    pos_blocks = seq // TM
    tab = pl.BlockSpec((TM, LANES), lambda i: (i % pos_blocks, 0))
    kv_shape = jax.ShapeDtypeStruct((t, N_KV_HEADS * LANES), BF16)
    return pl.pallas_call(
        _qkv_kernel,
        grid=(t // TM,),
        in_specs=[
            pl.BlockSpec((TM, D_MODEL), lambda i: (i, 0)),
            pl.BlockSpec((None, 1, D_MODEL), lambda i: (layer, 0, 0)),
            pl.BlockSpec((D_MODEL, wide), lambda i: (0, 0)),
            pl.BlockSpec((None, 1, wide), lambda i: (mixer, 0, 0)),
            tab, tab, tab,
        ],
        out_specs=[
            pl.BlockSpec((TM, QW), lambda i: (i, 0)),
            pl.BlockSpec((TM, N_KV_HEADS * LANES), lambda i: (i, 0)),
            pl.BlockSpec((TM, N_KV_HEADS * LANES), lambda i: (i, 0)),
        ],
        out_shape=[jax.ShapeDtypeStruct((t, QW), BF16), kv_shape, kv_shape],
        compiler_params=_cparams(("parallel",)),
        name="qkv",
    )(x, g, w, b, ca, cp, cm)


def _attn_kernel(sink_ref, q_ref, kp_ref, kc_ref, vp_ref, vc_ref, o_ref, kbuf_ref, vbuf_ref, *,
                 tiles_per_seq):
    first_tile = (pl.program_id(0) % tiles_per_seq) == 0
    kbuf_ref[:WINDOW] = kp_ref[...]
    kbuf_ref[WINDOW:] = kc_ref[...]
    vbuf_ref[:WINDOW] = vp_ref[...]
    vbuf_ref[WINDOW:] = vc_ref[...]

    t_idx = lax.broadcasted_iota(jnp.int32, (WINDOW, 2 * WINDOW), 0)
    s_idx = lax.broadcasted_iota(jnp.int32, (WINDOW, 2 * WINDOW), 1)
    diff = t_idx + WINDOW - s_idx
    band = (diff >= 0) & (diff < WINDOW)
    lane = lax.broadcasted_iota(jnp.int32, (WINDOW, LANES), 1)
    low = lane < HEAD_DIM
    zero = jnp.zeros((), BF16)
    sink_col = lax.broadcasted_iota(jnp.int32, (1, 2 * WINDOW), 1) == 0
    sink_row = lax.broadcasted_iota(jnp.int32, (2 * WINDOW, LANES), 0) == 0

    def block(r, carry):
        row0 = pl.multiple_of(r * WINDOW, WINDOW)
        q_rows = pl.ds(row0, WINDOW)
        kv_rows = pl.ds(row0, 2 * WINDOW)
        has_prev = (r > 0) | jnp.logical_not(first_tile)
        valid = band & ((s_idx >= WINDOW) | has_prev)
        for kh in range(N_KV_HEADS):
            ks = slice(kh * LANES, (kh + 1) * LANES)
            k2 = kbuf_ref[kv_rows, ks]
            v2 = jnp.where(sink_row, zero, vbuf_ref[kv_rows, ks])
            for pp in range(Q_PER_KV // 2):
                p = kh * (Q_PER_KV // 2) + pp
                cols = slice(p * LANES, (p + 1) * LANES)
                qb = q_ref[q_rows, cols]
                halves = []
                for hh in range(2):
                    head = 2 * p + hh
                    qm = jnp.where(low if hh == 0 else ~low, qb, zero)
                    s = lax.dot_general(qm, k2, (((1,), (1,)), ((), ())),
                                        preferred_element_type=F32)
                    fill = jnp.where(sink_col, sink_ref[head] * LOG2_E, -jnp.inf)
                    s = jnp.where(valid, s, fill)
                    e = jnp.exp2(s - jnp.max(s, axis=-1, keepdims=True))
                    denom = jnp.sum(e, axis=-1, keepdims=True)
                    halves.append(_dot(e.astype(BF16), v2) * (1.0 / denom))
                o_ref[q_rows, cols] = jnp.where(low, halves[0], halves[1]).astype(BF16)
        return carry

    lax.fori_loop(0, TM // WINDOW, block, 0)


def _attn(sinks, q, k2, v2, seq):
    t = q.shape[0]
    kvw = N_KV_HEADS * LANES
    blocks_per_tile = TM // WINDOW
    cur = lambda i: (i, 0)
    prev = lambda i: (jnp.maximum(i * blocks_per_tile - 1, 0), 0)
    return pl.pallas_call(
        functools.partial(_attn_kernel, tiles_per_seq=seq // TM),
        grid=(t // TM,),
        in_specs=[
            pl.BlockSpec(memory_space=pltpu.SMEM),
            pl.BlockSpec((TM, QW), cur),
            pl.BlockSpec((WINDOW, kvw), prev),
            pl.BlockSpec((TM, kvw), cur),
            pl.BlockSpec((WINDOW, kvw), prev),
            pl.BlockSpec((TM, kvw), cur),
        ],
        out_specs=pl.BlockSpec((TM, QW), cur),
        out_shape=jax.ShapeDtypeStruct((t, QW), BF16),
        scratch_shapes=[pltpu.VMEM((WINDOW + TM, kvw), BF16), pltpu.VMEM((WINDOW + TM, kvw), BF16)],
        compiler_params=_cparams(("parallel",)),
        name="attn",
    )(sinks, q, k2, k2, v2, v2)


def _wo_kernel(x_ref, a_ref, w_ref, b_ref, o_ref):
    o_ref[...] = x_ref[...] + _dot(a_ref[...], w_ref[...]) + b_ref[...]


def _wo(x, a, w, b, mixer):
    t = x.shape[0]
    return pl.pallas_call(
        _wo_kernel,
        grid=(t // TM,),
        in_specs=[
            pl.BlockSpec((TM, D_MODEL), lambda i: (i, 0)),
            pl.BlockSpec((TM, QW), lambda i: (i, 0)),
            pl.BlockSpec((QW, D_MODEL), lambda i: (0, 0)),
            pl.BlockSpec((None, 1, D_MODEL), lambda i: (mixer, 0, 0)),
        ],
        out_specs=pl.BlockSpec((TM, D_MODEL), lambda i: (i, 0)),
        out_shape=jax.ShapeDtypeStruct((t, D_MODEL), F32),
        compiler_params=_cparams(("parallel",)),
        name="wo",
    )(x, a, w, b)


def _ple_kernel(x_ref, p_ref, g_ref, wg_ref, wp_ref, fg_ref, o_ref, *, final):
    xf = x_ref[...]
    hg, rstd = _rms_split(xf, g_ref[...])
    gate = jax.nn.sigmoid(_dot(hg, wg_ref[...]) * rstd)
    proj = _dot(p_ref[...].astype(BF16), wp_ref[...])
    y = xf + gate * proj
    if final:
        y = _rms(y, fg_ref[...])
    o_ref[...] = y


def _ple(x, p, g, wg, wp, fg, layer, final):
    t = x.shape[0]
    return pl.pallas_call(
        functools.partial(_ple_kernel, final=final),
        grid=(t // TM,),
        in_specs=[
            pl.BlockSpec((TM, D_MODEL), lambda i: (i, 0)),
            pl.BlockSpec((None, TM, PLE_DIM), lambda i: (layer, i, 0)),
            pl.BlockSpec((None, 1, D_MODEL), lambda i: (layer, 0, 0)),
            pl.BlockSpec((D_MODEL, D_MODEL), lambda i: (0, 0)),
            pl.BlockSpec((PLE_DIM, D_MODEL), lambda i: (0, 0)),
            pl.BlockSpec((1, D_MODEL), lambda i: (0, 0)),
        ],
        out_specs=pl.BlockSpec((TM, D_MODEL), lambda i: (i, 0)),
        out_shape=jax.ShapeDtypeStruct((t, D_MODEL), F32),
        compiler_params=_cparams(("parallel",)),
        name="ple_final" if final else "ple",
    )(x, p, g, wg, wp, fg)


def _rope_coefficients(seq):
    half = ROPE_DIM // 2
    inv_freq = np.float32(ROPE_THETA) ** (-np.arange(0, ROPE_DIM, 2, dtype=np.float32) / np.float32(ROPE_DIM))
    ang = (np.arange(seq, dtype=np.float32)[:, None] * inv_freq[None, :]).astype(np.float32)
    cos = np.cos(ang.astype(np.float64)).astype(np.float32)
    sin = np.sin(ang.astype(np.float64)).astype(np.float32)
    ones = np.ones((seq, HEAD_DIM - ROPE_DIM), np.float32)
    zeros = np.zeros((seq, HEAD_DIM - half), np.float32)
    ca = np.concatenate([cos, cos, ones], axis=-1)
    cp = np.concatenate([-sin, zeros], axis=-1)
    cm = np.concatenate([np.zeros((seq, half), np.float32), sin, zeros[:, half:]], axis=-1)
    tile = lambda a: jnp.asarray(np.concatenate([a] * (LANES // HEAD_DIM), axis=-1))
    return tile(ca), tile(cp), tile(cm)


def kernel(x, p, ffn1_norm, ffn1_w1, ffn1_w3, ffn1_w2, mix_norm, ffn2_norm, ffn2_w1, ffn2_w3, ffn2_w2, ple_norm, ple_w_gate, ple_w_proj, gmlp_w_in, gmlp_ln_g, gmlp_ln_b, gmlp_w_s, gmlp_b_s, gmlp_w_out, swa_wq, swa_bq, swa_wk, swa_bk, swa_wv, swa_bv, swa_sinks, swa_wo, swa_bo, final_norm):
    batch, seq, _ = x.shape
    t = batch * seq
    assert seq % TM == 0 and TM % WINDOW == 0 and t % TM_FFN == 0
    rows = lambda a: a.reshape(a.shape[0], 1, a.shape[1])
    xs = x.reshape(t, D_MODEL)
    ps = p.reshape(DEPTH, t, PLE_DIM)
    ca, cp, cm = _rope_coefficients(seq)

    ffn1_g, ffn2_g, mix_g, ple_g = rows(ffn1_norm), rows(ffn2_norm), rows(mix_norm), rows(ple_norm)
    swa_bqkv = rows(jnp.concatenate([swa_bq, swa_bk, swa_bv], axis=1))
    fg = final_norm.reshape(1, D_MODEL)

    for i in range(DEPTH):
        j = i // 2
        if i % 2 == 0:
            mixer_jobs = [_CastJob(j, (gmlp_w_in,)), _CastJob(j, (gmlp_w_out,))]
        else:
            mixer_jobs = [_CastJob(j, (swa_wq, swa_wk, swa_wv)), _CastJob(j, (swa_wo,))]
        xs, (mix_w0, mix_w1) = _ffn(xs, ffn1_g, ffn1_w1, ffn1_w3, ffn1_w2, layer=i, jobs=mixer_jobs)
        if i % 2 == 0:
            b_s = jnp.broadcast_to(gmlp_b_s[j][:, :, None], (GMLP_GROUPS, CHUNK, GMLP_GROUP_DIM))
            xs = _gmlp(xs, mix_g, mix_w0,
                       gmlp_ln_g[j].reshape(N_HALF, 1, TG), gmlp_ln_b[j].reshape(N_HALF, 1, TG),
                       gmlp_w_s, b_s, mix_w1, layer=i, mixer=j)
        else:
            q, k2, v2 = _qkv(xs, mix_g, mix_w0, swa_bqkv, ca, cp, cm, seq, layer=i, mixer=j)
            a = _attn(swa_sinks[j], q, k2, v2, seq)
            xs = _wo(xs, a, mix_w1, rows(swa_bo), mixer=j)
        ple_jobs = [_CastJob(i, (ple_w_gate,)), _CastJob(i, (ple_w_proj,))]
        xs, (ple_wg, ple_wp) = _ffn(xs, ffn2_g, ffn2_w1, ffn2_w3, ffn2_w2, layer=i, jobs=ple_jobs)
        xs = _ple(xs, ps, ple_g, ple_wg, ple_wp, fg, layer=i, final=(i == DEPTH - 1))
    return xs.reshape(batch, seq, D_MODEL)
```

```python
import functools
import math
from typing import NamedTuple

import jax
import jax.numpy as jnp
import numpy as np
from jax import lax
from jax.experimental import pallas as pl
from jax.experimental.pallas import tpu as pltpu

F32 = jnp.float32
BF16 = jnp.bfloat16

D_MODEL = 2048
DEPTH = 2
D_FF = 5632
PLE_DIM = 256
RMS_EPS = 1e-6
LN_EPS = 1e-5
CHUNK = 128
GMLP_WIDTH = 2 * D_MODEL
GMLP_GROUPS = 16
GMLP_GROUP_DIM = GMLP_WIDTH // GMLP_GROUPS
N_Q_HEADS = 32
N_KV_HEADS = 4
HEAD_DIM = 64
Q_PER_KV = N_Q_HEADS // N_KV_HEADS
WINDOW = 128
ROPE_THETA = 500000.0
ROPE_DIM = HEAD_DIM // 4
LOG2_E = 1.4426950408889634
QW = N_Q_HEADS * HEAD_DIM
KW = N_KV_HEADS * HEAD_DIM

LANES = 128
BF16_SUBLANES = 16

TM = 512
TM_FFN = 1024
TF = 512
TF_HEAD = 512
N_CAST_TILES = 8
TN = 512
TG = 1024
N_HALF = GMLP_WIDTH // TG
GROUPS_PER_SLAB = TG // GMLP_GROUP_DIM
VMEM_LIMIT = 63 * 1024 * 1024


def _cparams(sem):
    return pltpu.CompilerParams(dimension_semantics=sem, vmem_limit_bytes=VMEM_LIMIT)


def _dot(a, b):
    return jnp.dot(a, b, preferred_element_type=F32)


def _rms(xf, g):
    ms = jnp.mean(xf * xf, axis=-1, keepdims=True)
    return xf * lax.rsqrt(ms + RMS_EPS) * g


def _rms_split(xf, g):
    ms = jnp.mean(xf * xf, axis=-1, keepdims=True)
    return (xf * g).astype(BF16), lax.rsqrt(ms + RMS_EPS)


def _gelu(z):
    return 0.5 * z * (1.0 + lax.erf(z * (1.0 / math.sqrt(2.0))))


def _cast_block(*refs):
    *in_refs, o_ref = refs
    col = 0
    for r in in_refs:
        w = r.shape[-1]
        o_ref[:, col:col + w] = r[...].astype(BF16)
        col += w


class _CastJob(NamedTuple):
    layer: int
    arrays: tuple


def _cast_job_blocking(rows, n_tiles, n_steps):
    per_tile = rows // n_tiles
    for n_sub in range(n_steps, 0, -1):
        if per_tile % n_sub == 0 and (per_tile // n_sub) % BF16_SUBLANES == 0:
            return per_tile // n_sub, n_sub
    raise ValueError(f"cannot spread a {rows}-row cast over {n_tiles} tiles")


def _ffn_step(f, x_ref, g_ref, weights, o_ref, h_ref, rs_ref, tail):
    def swiglu(hg, rstd, w):
        a = _dot(hg, w[0]) * rstd
        b = _dot(hg, w[1]) * rstd
        gate = (0.5 * (a * jax.nn.sigmoid(a)) * b).astype(BF16)
        return _dot(gate, w[2])

    @pl.when(f == 0)
    def _():
        w = weights()
        xf = x_ref[...]
        hg, rstd = _rms_split(xf, g_ref[...])
        h_ref[...] = hg
        rs_ref[...] = rstd
        o_ref[...] = xf + swiglu(hg, rstd, w)
        tail(w)

    @pl.when(f > 0)
    def _():
        w = weights()
        o_ref[...] += swiglu(h_ref[...], rs_ref[...], w)
        tail(w)


def _ffn_head_kernel(x_ref, g_ref, w1_ref, w3_ref, w2_ref,
                     o_ref, w1b_ref, w3b_ref, w2b_ref, h_ref, rs_ref):
    def weights():
        return tuple(r[...].astype(BF16) for r in (w1_ref, w3_ref, w2_ref))

    def store_bf16(w):
        w1b_ref[...], w3b_ref[...], w2b_ref[...] = w

    _ffn_step(pl.program_id(0), x_ref, g_ref, weights, o_ref, h_ref, rs_ref, store_bf16)


def _ffn_body_kernel(*refs, job_sizes):
    n_cast_in = sum(job_sizes)
    x_ref, g_ref, w1_ref, w3_ref, w2_ref = refs[:5]
    cast_in = refs[5:5 + n_cast_in]
    o_ref = refs[6 + n_cast_in]
    cast_out = refs[7 + n_cast_in:7 + n_cast_in + len(job_sizes)]
    h_ref, rs_ref = refs[-2:]

    def weights():
        return w1_ref[...], w3_ref[...], w2_ref[...]

    def side_casts(_):
        start = 0
        for n, out_ref in zip(job_sizes, cast_out):
            _cast_block(*cast_in[start:start + n], out_ref)
            start += n

    _ffn_step(pl.program_id(1), x_ref, g_ref, weights, o_ref, h_ref, rs_ref, side_casts)


def _ffn(x, g, w1, w3, w2, layer, jobs):
    t = x.shape[0]
    n_tiles, n_steps, n_head_steps = t // TM_FFN, D_FF // TF, D_FF // TF_HEAD
    scratch = [pltpu.VMEM((TM_FFN, D_MODEL), BF16), pltpu.VMEM((TM_FFN, 1), F32)]
    y_shape = jax.ShapeDtypeStruct((t, D_MODEL), F32)
    wide = jax.ShapeDtypeStruct((D_MODEL, D_FF), BF16)
    tall = jax.ShapeDtypeStruct((D_FF, D_MODEL), BF16)

    y_head, w1b, w3b, w2b = pl.pallas_call(
        _ffn_head_kernel,
        grid=(n_head_steps,),
        in_specs=[
            pl.BlockSpec((TM_FFN, D_MODEL), lambda f: (0, 0), pipeline_mode=pl.Buffered(1)),
            pl.BlockSpec((None, 1, D_MODEL), lambda f: (layer, 0, 0)),
            pl.BlockSpec((None, D_MODEL, TF_HEAD), lambda f: (layer, 0, f)),
            pl.BlockSpec((None, D_MODEL, TF_HEAD), lambda f: (layer, 0, f)),
            pl.BlockSpec((None, TF_HEAD, D_MODEL), lambda f: (layer, f, 0)),
        ],
        out_specs=[
            pl.BlockSpec((TM_FFN, D_MODEL), lambda f: (0, 0), pipeline_mode=pl.Buffered(1)),
            pl.BlockSpec((D_MODEL, TF_HEAD), lambda f: (0, f)),
            pl.BlockSpec((D_MODEL, TF_HEAD), lambda f: (0, f)),
            pl.BlockSpec((TF_HEAD, D_MODEL), lambda f: (f, 0)),
        ],
        out_shape=[y_shape, wide, wide, tall],
        scratch_shapes=scratch,
        compiler_params=_cparams(("arbitrary",)),
        name="ffn_head",
    )(x, g, w1, w3, w2)

    in_specs = [
        pl.BlockSpec((TM_FFN, D_MODEL), lambda i, f: (i + 1, 0)),
        pl.BlockSpec((None, 1, D_MODEL), lambda i, f: (layer, 0, 0)),
        pl.BlockSpec((D_MODEL, TF), lambda i, f: (0, f)),
        pl.BlockSpec((D_MODEL, TF), lambda i, f: (0, f)),
        pl.BlockSpec((TF, D_MODEL), lambda i, f: (f, 0)),
    ]
    out_specs = [pl.BlockSpec((TM_FFN, D_MODEL), lambda i, f: (i + 1, 0))]
    out_shape = [y_shape]
    cast_inputs = []
    for job in jobs:
        rows = job.arrays[0].shape[1]
        cols = sum(a.shape[2] for a in job.arrays)
        rb, n_sub = _cast_job_blocking(rows, N_CAST_TILES, n_steps)

        def block(i, f, n_sub=n_sub):
            moving = i * n_sub + jnp.minimum(f, n_sub - 1)
            return jnp.where(i < N_CAST_TILES, moving, N_CAST_TILES * n_sub - 1)

        for a in job.arrays:
            in_specs.append(pl.BlockSpec(
                (None, rb, a.shape[2]),
                lambda i, f, block=block, l=job.layer: (l, block(i, f), 0)))
            cast_inputs.append(a)
        out_specs.append(pl.BlockSpec((rb, cols), lambda i, f, block=block: (block(i, f), 0)))
        out_shape.append(jax.ShapeDtypeStruct((rows, cols), BF16))
    in_specs.append(pl.BlockSpec(memory_space=pl.ANY))
    y, *cast = pl.pallas_call(
        functools.partial(_ffn_body_kernel, job_sizes=tuple(len(job.arrays) for job in jobs)),
        grid=(n_tiles - 1, n_steps),
        in_specs=in_specs,
        out_specs=out_specs,
        out_shape=out_shape,
        input_output_aliases={len(in_specs) - 1: 0},
        scratch_shapes=scratch,
        compiler_params=_cparams(("arbitrary", "arbitrary")),
        name="ffn",
    )(x, g, w1b, w3b, w2b, *cast_inputs, y_head)
    return y, cast


def _gmlp_kernel(x_ref, g_ref, win_ref, lng_ref, lnb_ref, ws_ref, bs_ref, wout_ref,
                 o_ref, h_ref, v_ref, s1_ref, s2_ref, mu_ref, rs_ref, gt0_ref, gt1_ref):
    j = pl.program_id(1)
    gt_refs = (gt0_ref, gt1_ref)

    def v_slab(h):
        z = _gelu(_dot(h, win_ref[...]))
        v_ref[j] = z
        p1 = z[:, :LANES]
        p2 = p1 * p1
        for c in range(1, TG // LANES):
            zc = z[:, c * LANES:(c + 1) * LANES]
            p1 += zc
            p2 += zc * zc
        return p1, p2

    @pl.when(j == 0)
    def _():
        xf = x_ref[...]
        o_ref[...] = xf
        h = _rms(xf, g_ref[...]).astype(BF16)
        h_ref[...] = h
        s1_ref[...], s2_ref[...] = v_slab(h)

    @pl.when((j > 0) & (j < N_HALF))
    def _():
        p1, p2 = v_slab(h_ref[...])
        s1_ref[...] += p1
        s2_ref[...] += p2

    def gate(k, dst_ref):
        u = _gelu(_dot(h_ref[...], win_ref[...]))
        if isinstance(k, int) and k == 0:
            mu = jnp.sum(s1_ref[...], axis=-1, keepdims=True) * (1.0 / GMLP_WIDTH)
            ex2 = jnp.sum(s2_ref[...], axis=-1, keepdims=True) * (1.0 / GMLP_WIDTH)
            rs = lax.rsqrt(ex2 - mu * mu + LN_EPS)
            mu_ref[...] = mu
            rs_ref[...] = rs
        else:
            mu, rs = mu_ref[...], rs_ref[...]
        vn = ((v_ref[k] - mu) * rs * lng_ref[k] + lnb_ref[k]).astype(BF16)
        row = lax.broadcasted_iota(jnp.int32, (CHUNK, CHUNK), 0)
        col = lax.broadcasted_iota(jnp.int32, (CHUNK, CHUNK), 1)
        causal = row >= col
        for gg in range(GROUPS_PER_SLAB):
            grp = GROUPS_PER_SLAB * k + gg
            w = jnp.where(causal, ws_ref[grp], 0.0).astype(BF16)
            bias = bs_ref[grp]
            cols = slice(gg * GMLP_GROUP_DIM, (gg + 1) * GMLP_GROUP_DIM)
            for c in range(TM // CHUNK):
                rows = slice(c * CHUNK, (c + 1) * CHUNK)
                s = _dot(w, vn[rows, cols]) + bias
                dst_ref[rows, cols] = (u[rows, cols] * s).astype(BF16)

    def project(src_ref):
        o_ref[...] += _dot(src_ref[...], wout_ref[...])

    @pl.when(j == N_HALF)
    def _():
        gate(0, gt_refs[0])

    for parity in range(2):
        @pl.when((j > N_HALF) & (j < 2 * N_HALF) & (((j - N_HALF) & 1) == parity))
        def _():
            project(gt_refs[1 - parity])
            gate(j - N_HALF, gt_refs[parity])

    @pl.when(j == 2 * N_HALF)
    def _():
        project(gt_refs[(N_HALF - 1) % 2])


def _gmlp(x, g, w_in, ln_g, ln_b, w_s, b_s, w_out, layer, mixer):
    t = x.shape[0]

    def win_map(i, j):
        return (0, jnp.where(j < N_HALF, j + N_HALF, jnp.minimum(j, 2 * N_HALF - 1) - N_HALF))

    def wout_map(i, j):
        return (jnp.where(j <= N_HALF, N_HALF - 1, j - N_HALF - 1), 0)

    def x_map(i, j):
        return (jnp.minimum(i + (j >= 2 * N_HALF - 1).astype(jnp.int32), t // TM - 1), 0)

    return pl.pallas_call(
        _gmlp_kernel,
        grid=(t // TM, 2 * N_HALF + 1),
        in_specs=[
            pl.BlockSpec((TM, D_MODEL), x_map),
            pl.BlockSpec((None, 1, D_MODEL), lambda i, j: (layer, 0, 0)),
            pl.BlockSpec((D_MODEL, TG), win_map),
            pl.BlockSpec((N_HALF, 1, TG), lambda i, j: (0, 0, 0)),
            pl.BlockSpec((N_HALF, 1, TG), lambda i, j: (0, 0, 0)),
            pl.BlockSpec((None, GMLP_GROUPS, CHUNK, CHUNK), lambda i, j: (mixer, 0, 0, 0)),
            pl.BlockSpec((GMLP_GROUPS, CHUNK, GMLP_GROUP_DIM), lambda i, j: (0, 0, 0)),
            pl.BlockSpec((TG, D_MODEL), wout_map),
        ],
        out_specs=pl.BlockSpec((TM, D_MODEL), lambda i, j: (i, 0)),
        out_shape=jax.ShapeDtypeStruct((t, D_MODEL), F32),
        scratch_shapes=[
            pltpu.VMEM((TM, D_MODEL), BF16),
            pltpu.VMEM((N_HALF, TM, TG), F32),
            pltpu.VMEM((TM, LANES), F32),
            pltpu.VMEM((TM, LANES), F32),
            pltpu.VMEM((TM, 1), F32),
            pltpu.VMEM((TM, 1), F32),
            pltpu.VMEM((TM, TG), BF16),
            pltpu.VMEM((TM, TG), BF16),
        ],
        compiler_params=_cparams(("parallel", "arbitrary")),
        name="gmlp",
    )(x, g, w_in, ln_g, ln_b, w_s, b_s, w_out)


def _rope(y, ca, cp, cm):
    n = y.shape[-1]
    reps = n // LANES
    ca = jnp.concatenate([ca] * reps, axis=-1)
    cp = jnp.concatenate([cp] * reps, axis=-1)
    cm = jnp.concatenate([cm] * reps, axis=-1)
    half = ROPE_DIM // 2
    return y * ca + pltpu.roll(y, n - half, 1) * cp + pltpu.roll(y, half, 1) * cm


def _dup_heads(y):
    lane = lax.broadcasted_iota(jnp.int32, (y.shape[0], LANES), 1)
    low = lane < HEAD_DIM
    out = []
    for p in range(y.shape[-1] // LANES):
        blk = y[:, p * LANES:(p + 1) * LANES]
        swapped = pltpu.roll(blk, HEAD_DIM, 1)
        out.append(jnp.where(low, blk, swapped))
        out.append(jnp.where(low, swapped, blk))
    return jnp.concatenate(out, axis=-1)


def _qkv_kernel(x_ref, g_ref, w_ref, b_ref, ca_ref, cp_ref, cm_ref, q_ref, k_ref, v_ref):
    hg, rstd = _rms_split(x_ref[...], g_ref[...])
    ca, cp, cm = ca_ref[...], cp_ref[...], cm_ref[...]
    for n in range(QW // TN):
        cols = slice(n * TN, (n + 1) * TN)
        y = _dot(hg, w_ref[:, cols]) * rstd + b_ref[:, cols]
        q_ref[:, cols] = (_rope(y, ca, cp, cm) * (HEAD_DIM ** -0.5 * LOG2_E)).astype(BF16)
    y = _dot(hg, w_ref[:, QW:]) * rstd + b_ref[:, QW:]
    k_ref[...] = _dup_heads(_rope(y[:, :KW], ca, cp, cm)).astype(BF16)
    v_ref[...] = _dup_heads(y[:, KW:]).astype(BF16)


def _qkv(x, g, w, b, ca, cp, cm, seq, layer, mixer):
    t = x.shape[0]
    wide = QW + 2 * KW
    pos_blocks = seq // TM
    tab = pl.BlockSpec((TM, LANES), lambda i: (i % pos_blocks, 0))
    kv_shape = jax.ShapeDtypeStruct((t, N_KV_HEADS * LANES), BF16)
    return pl.pallas_call(
        _qkv_kernel,
        grid=(t // TM,),
        in_specs=[
            pl.BlockSpec((TM, D_MODEL), lambda i: (i, 0)),
            pl.BlockSpec((None, 1, D_MODEL), lambda i: (layer, 0, 0)),
            pl.BlockSpec((D_MODEL, wide), lambda i: (0, 0)),
            pl.BlockSpec((None, 1, wide), lambda i: (mixer, 0, 0)),
            tab, tab, tab,
        ],
        out_specs=[
            pl.BlockSpec((TM, QW), lambda i: (i, 0)),
            pl.BlockSpec((TM, N_KV_HEADS * LANES), lambda i: (i, 0)),
            pl.BlockSpec((TM, N_KV_HEADS * LANES), lambda i: (i, 0)),
        ],
        out_shape=[jax.ShapeDtypeStruct((t, QW), BF16), kv_shape, kv_shape],
        compiler_params=_cparams(("parallel",)),
        name="qkv",
    )(x, g, w, b, ca, cp, cm)


def _attn_kernel(sink_ref, q_ref, kp_ref, kc_ref, vp_ref, vc_ref, o_ref, kbuf_ref, vbuf_ref, *,
                 tiles_per_seq):
    first_tile = (pl.program_id(0) % tiles_per_seq) == 0
    kbuf_ref[:WINDOW] = kp_ref[...]
    kbuf_ref[WINDOW:] = kc_ref[...]
    vbuf_ref[:WINDOW] = vp_ref[...]
    vbuf_ref[WINDOW:] = vc_ref[...]

    t_idx = lax.broadcasted_iota(jnp.int32, (WINDOW, 2 * WINDOW), 0)
    s_idx = lax.broadcasted_iota(jnp.int32, (WINDOW, 2 * WINDOW), 1)
    diff = t_idx + WINDOW - s_idx
    band = (diff >= 0) & (diff < WINDOW)
    lane = lax.broadcasted_iota(jnp.int32, (WINDOW, LANES), 1)
    low = lane < HEAD_DIM
    zero = jnp.zeros((), BF16)
    sink_col = lax.broadcasted_iota(jnp.int32, (1, 2 * WINDOW), 1) == 0
    sink_row = lax.broadcasted_iota(jnp.int32, (2 * WINDOW, LANES), 0) == 0

    def block(r, carry):
        row0 = pl.multiple_of(r * WINDOW, WINDOW)
        q_rows = pl.ds(row0, WINDOW)
        kv_rows = pl.ds(row0, 2 * WINDOW)
        has_prev = (r > 0) | jnp.logical_not(first_tile)
        valid = band & ((s_idx >= WINDOW) | has_prev)
        for kh in range(N_KV_HEADS):
            ks = slice(kh * LANES, (kh + 1) * LANES)
            k2 = kbuf_ref[kv_rows, ks]
            v2 = jnp.where(sink_row, zero, vbuf_ref[kv_rows, ks])
            for pp in range(Q_PER_KV // 2):
                p = kh * (Q_PER_KV // 2) + pp
                cols = slice(p * LANES, (p + 1) * LANES)
                qb = q_ref[q_rows, cols]
                halves = []
                for hh in range(2):
                    head = 2 * p + hh
                    qm = jnp.where(low if hh == 0 else ~low, qb, zero)
                    s = lax.dot_general(qm, k2, (((1,), (1,)), ((), ())),
                                        preferred_element_type=F32)
                    fill = jnp.where(sink_col, sink_ref[head] * LOG2_E, -jnp.inf)
                    s = jnp.where(valid, s, fill)
                    e = jnp.exp2(s - jnp.max(s, axis=-1, keepdims=True))
                    denom = jnp.sum(e, axis=-1, keepdims=True)
                    halves.append(_dot(e.astype(BF16), v2) * (1.0 / denom))
                o_ref[q_rows, cols] = jnp.where(low, halves[0], halves[1]).astype(BF16)
        return carry

    lax.fori_loop(0, TM // WINDOW, block, 0)


def _attn(sinks, q, k2, v2, seq):
    t = q.shape[0]
    kvw = N_KV_HEADS * LANES
    blocks_per_tile = TM // WINDOW
    cur = lambda i: (i, 0)
    prev = lambda i: (jnp.maximum(i * blocks_per_tile - 1, 0), 0)
    return pl.pallas_call(
        functools.partial(_attn_kernel, tiles_per_seq=seq // TM),
        grid=(t // TM,),
        in_specs=[
            pl.BlockSpec(memory_space=pltpu.SMEM),
            pl.BlockSpec((TM, QW), cur),
            pl.BlockSpec((WINDOW, kvw), prev),
            pl.BlockSpec((TM, kvw), cur),
            pl.BlockSpec((WINDOW, kvw), prev),
            pl.BlockSpec((TM, kvw), cur),
        ],
        out_specs=pl.BlockSpec((TM, QW), cur),
        out_shape=jax.ShapeDtypeStruct((t, QW), BF16),
        scratch_shapes=[pltpu.VMEM((WINDOW + TM, kvw), BF16), pltpu.VMEM((WINDOW + TM, kvw), BF16)],
        compiler_params=_cparams(("parallel",)),
        name="attn",
    )(sinks, q, k2, k2, v2, v2)


def _wo_kernel(x_ref, a_ref, w_ref, b_ref, o_ref):
    o_ref[...] = x_ref[...] + _dot(a_ref[...], w_ref[...]) + b_ref[...]


def _wo(x, a, w, b, mixer):
    t = x.shape[0]
    return pl.pallas_call(
        _wo_kernel,
        grid=(t // TM,),
        in_specs=[
            pl.BlockSpec((TM, D_MODEL), lambda i: (i, 0)),
            pl.BlockSpec((TM, QW), lambda i: (i, 0)),
            pl.BlockSpec((QW, D_MODEL), lambda i: (0, 0)),
            pl.BlockSpec((None, 1, D_MODEL), lambda i: (mixer, 0, 0)),
        ],
        out_specs=pl.BlockSpec((TM, D_MODEL), lambda i: (i, 0)),
        out_shape=jax.ShapeDtypeStruct((t, D_MODEL), F32),
        compiler_params=_cparams(("parallel",)),
        name="wo",
    )(x, a, w, b)


def _ple_kernel(x_ref, p_ref, g_ref, wg_ref, wp_ref, fg_ref, o_ref, *, final):
    xf = x_ref[...]
    hg, rstd = _rms_split(xf, g_ref[...])
    gate = jax.nn.sigmoid(_dot(hg, wg_ref[...]) * rstd)
    proj = _dot(p_ref[...].astype(BF16), wp_ref[...])
    y = xf + gate * proj
    if final:
        y = _rms(y, fg_ref[...])
    o_ref[...] = y


def _ple(x, p, g, wg, wp, fg, layer, final):
    t = x.shape[0]
    return pl.pallas_call(
        functools.partial(_ple_kernel, final=final),
        grid=(t // TM,),
        in_specs=[
            pl.BlockSpec((TM, D_MODEL), lambda i: (i, 0)),
            pl.BlockSpec((None, TM, PLE_DIM), lambda i: (layer, i, 0)),
            pl.BlockSpec((None, 1, D_MODEL), lambda i: (layer, 0, 0)),
            pl.BlockSpec((D_MODEL, D_MODEL), lambda i: (0, 0)),
            pl.BlockSpec((PLE_DIM, D_MODEL), lambda i: (0, 0)),
            pl.BlockSpec((1, D_MODEL), lambda i: (0, 0)),
        ],
        out_specs=pl.BlockSpec((TM, D_MODEL), lambda i: (i, 0)),
        out_shape=jax.ShapeDtypeStruct((t, D_MODEL), F32),
        compiler_params=_cparams(("parallel",)),
        name="ple_final" if final else "ple",
    )(x, p, g, wg, wp, fg)


def _rope_coefficients(seq):
    half = ROPE_DIM // 2
    inv_freq = np.float32(ROPE_THETA) ** (-np.arange(0, ROPE_DIM, 2, dtype=np.float32) / np.float32(ROPE_DIM))
    ang = (np.arange(seq, dtype=np.float32)[:, None] * inv_freq[None, :]).astype(np.float32)
    cos = np.cos(ang.astype(np.float64)).astype(np.float32)
    sin = np.sin(ang.astype(np.float64)).astype(np.float32)
    ones = np.ones((seq, HEAD_DIM - ROPE_DIM), np.float32)
    zeros = np.zeros((seq, HEAD_DIM - half), np.float32)
    ca = np.concatenate([cos, cos, ones], axis=-1)
    cp = np.concatenate([-sin, zeros], axis=-1)
    cm = np.concatenate([np.zeros((seq, half), np.float32), sin, zeros[:, half:]], axis=-1)
    tile = lambda a: jnp.asarray(np.concatenate([a] * (LANES // HEAD_DIM), axis=-1))
    return tile(ca), tile(cp), tile(cm)


def kernel(x, p, ffn1_norm, ffn1_w1, ffn1_w3, ffn1_w2, mix_norm, ffn2_norm, ffn2_w1, ffn2_w3, ffn2_w2, ple_norm, ple_w_gate, ple_w_proj, gmlp_w_in, gmlp_ln_g, gmlp_ln_b, gmlp_w_s, gmlp_b_s, gmlp_w_out, swa_wq, swa_bq, swa_wk, swa_bk, swa_wv, swa_bv, swa_sinks, swa_wo, swa_bo, final_norm):
    batch, seq, _ = x.shape
    t = batch * seq
    assert seq % TM == 0 and TM % WINDOW == 0 and t % TM_FFN == 0
    rows = lambda a: a.reshape(a.shape[0], 1, a.shape[1])
    xs = x.reshape(t, D_MODEL)
    ps = p.reshape(DEPTH, t, PLE_DIM)
    ca, cp, cm = _rope_coefficients(seq)

    ffn1_g, ffn2_g, mix_g, ple_g = rows(ffn1_norm), rows(ffn2_norm), rows(mix_norm), rows(ple_norm)
    swa_bqkv = rows(jnp.concatenate([swa_bq, swa_bk, swa_bv], axis=1))
    fg = final_norm.reshape(1, D_MODEL)

    for i in range(DEPTH):
        j = i // 2
        if i % 2 == 0:
            mixer_jobs = [_CastJob(j, (gmlp_w_in,)), _CastJob(j, (gmlp_w_out,))]
        else:
            mixer_jobs = [_CastJob(j, (swa_wq, swa_wk, swa_wv)), _CastJob(j, (swa_wo,))]
        xs, (mix_w0, mix_w1) = _ffn(xs, ffn1_g, ffn1_w1, ffn1_w3, ffn1_w2, layer=i, jobs=mixer_jobs)
        if i % 2 == 0:
            b_s = jnp.broadcast_to(gmlp_b_s[j][:, :, None], (GMLP_GROUPS, CHUNK, GMLP_GROUP_DIM))
            xs = _gmlp(xs, mix_g, mix_w0,
                       gmlp_ln_g[j].reshape(N_HALF, 1, TG), gmlp_ln_b[j].reshape(N_HALF, 1, TG),
                       gmlp_w_s, b_s, mix_w1, layer=i, mixer=j)
        else:
            q, k2, v2 = _qkv(xs, mix_g, mix_w0, swa_bqkv, ca, cp, cm, seq, layer=i, mixer=j)
            a = _attn(swa_sinks[j], q, k2, v2, seq)
            xs = _wo(xs, a, mix_w1, rows(swa_bo), mixer=j)
        ple_jobs = [_CastJob(i, (ple_w_gate,)), _CastJob(i, (ple_w_proj,))]
        xs, (ple_wg, ple_wp) = _ffn(xs, ffn2_g, ffn2_w1, ffn2_w3, ffn2_w2, layer=i, jobs=ple_jobs)
        xs = _ple(xs, ps, ple_g, ple_wg, ple_wp, fg, layer=i, final=(i == DEPTH - 1))
    return xs.reshape(batch, seq, D_MODEL)
```

```python
import functools
import math
from typing import NamedTuple

import jax
import jax.numpy as jnp
import numpy as np
from jax import lax
from jax.experimental import pallas as pl
from jax.experimental.pallas import tpu as pltpu

F32 = jnp.float32
BF16 = jnp.bfloat16

D_MODEL = 2048
DEPTH = 2
D_FF = 5632
PLE_DIM = 256
RMS_EPS = 1e-6
LN_EPS = 1e-5
CHUNK = 128
GMLP_WIDTH = 2 * D_MODEL
GMLP_GROUPS = 16
GMLP_GROUP_DIM = GMLP_WIDTH // GMLP_GROUPS
N_Q_HEADS = 32
N_KV_HEADS = 4
HEAD_DIM = 64
Q_PER_KV = N_Q_HEADS // N_KV_HEADS
WINDOW = 128
ROPE_THETA = 500000.0
ROPE_DIM = HEAD_DIM // 4
LOG2_E = 1.4426950408889634
QW = N_Q_HEADS * HEAD_DIM
KW = N_KV_HEADS * HEAD_DIM

LANES = 128
BF16_SUBLANES = 16

TM = 512
TM_FFN = 1024
TF = 512
TF_HEAD = 512
N_CAST_TILES = 8
TN = 512
TG = 1024
N_HALF = GMLP_WIDTH // TG
GROUPS_PER_SLAB = TG // GMLP_GROUP_DIM
VMEM_LIMIT = 63 * 1024 * 1024


def _cparams(sem):
    return pltpu.CompilerParams(dimension_semantics=sem, vmem_limit_bytes=VMEM_LIMIT)


def _dot(a, b):
    return jnp.dot(a, b, preferred_element_type=F32)


def _rms(xf, g):
    ms = jnp.mean(xf * xf, axis=-1, keepdims=True)
    return xf * lax.rsqrt(ms + RMS_EPS) * g


def _rms_split(xf, g):
    ms = jnp.mean(xf * xf, axis=-1, keepdims=True)
    return (xf * g).astype(BF16), lax.rsqrt(ms + RMS_EPS)


def _gelu(z):
    return 0.5 * z * (1.0 + lax.erf(z * (1.0 / math.sqrt(2.0))))


def _cast_block(*refs):
    *in_refs, o_ref = refs
    col = 0
    for r in in_refs:
        w = r.shape[-1]
        o_ref[:, col:col + w] = r[...].astype(BF16)
        col += w


class _CastJob(NamedTuple):
    layer: int
    arrays: tuple


def _cast_job_blocking(rows, n_tiles, n_steps):
    per_tile = rows // n_tiles
    for n_sub in range(n_steps, 0, -1):
        if per_tile % n_sub == 0 and (per_tile // n_sub) % BF16_SUBLANES == 0:
            return per_tile // n_sub, n_sub
    raise ValueError(f"cannot spread a {rows}-row cast over {n_tiles} tiles")


def _ffn_step(f, x_ref, g_ref, weights, o_ref, h_ref, rs_ref, tail):
    def swiglu(hg, rstd, w):
        a = _dot(hg, w[0]) * rstd
        b = _dot(hg, w[1]) * rstd
        gate = (0.5 * (a * jax.nn.sigmoid(a)) * b).astype(BF16)
        return _dot(gate, w[2])

    @pl.when(f == 0)
    def _():
        w = weights()
        xf = x_ref[...]
        hg, rstd = _rms_split(xf, g_ref[...])
        h_ref[...] = hg
        rs_ref[...] = rstd
        o_ref[...] = xf + swiglu(hg, rstd, w)
        tail(w)

    @pl.when(f > 0)
    def _():
        w = weights()
        o_ref[...] += swiglu(h_ref[...], rs_ref[...], w)
        tail(w)


def _ffn_head_kernel(x_ref, g_ref, w1_ref, w3_ref, w2_ref,
                     o_ref, w1b_ref, w3b_ref, w2b_ref, h_ref, rs_ref):
    def weights():
        return tuple(r[...].astype(BF16) for r in (w1_ref, w3_ref, w2_ref))

    def store_bf16(w):
        w1b_ref[...], w3b_ref[...], w2b_ref[...] = w

    _ffn_step(pl.program_id(0), x_ref, g_ref, weights, o_ref, h_ref, rs_ref, store_bf16)


def _ffn_body_kernel(*refs, job_sizes):
    n_cast_in = sum(job_sizes)
    x_ref, g_ref, w1_ref, w3_ref, w2_ref = refs[:5]
    cast_in = refs[5:5 + n_cast_in]
    o_ref = refs[6 + n_cast_in]
    cast_out = refs[7 + n_cast_in:7 + n_cast_in + len(job_sizes)]
    h_ref, rs_ref = refs[-2:]

    def weights():
        return w1_ref[...], w3_ref[...], w2_ref[...]

    def side_casts(_):
        start = 0
        for n, out_ref in zip(job_sizes, cast_out):
            _cast_block(*cast_in[start:start + n], out_ref)
            start += n

    _ffn_step(pl.program_id(1), x_ref, g_ref, weights, o_ref, h_ref, rs_ref, side_casts)


def _ffn(x, g, w1, w3, w2, layer, jobs):
    t = x.shape[0]
    n_tiles, n_steps, n_head_steps = t // TM_FFN, D_FF // TF, D_FF // TF_HEAD
    scratch = [pltpu.VMEM((TM_FFN, D_MODEL), BF16), pltpu.VMEM((TM_FFN, 1), F32)]
    y_shape = jax.ShapeDtypeStruct((t, D_MODEL), F32)
    wide = jax.ShapeDtypeStruct((D_MODEL, D_FF), BF16)
    tall = jax.ShapeDtypeStruct((D_FF, D_MODEL), BF16)

    y_head, w1b, w3b, w2b = pl.pallas_call(
        _ffn_head_kernel,
        grid=(n_head_steps,),
        in_specs=[
            pl.BlockSpec((TM_FFN, D_MODEL), lambda f: (0, 0), pipeline_mode=pl.Buffered(1)),
            pl.BlockSpec((None, 1, D_MODEL), lambda f: (layer, 0, 0)),
            pl.BlockSpec((None, D_MODEL, TF_HEAD), lambda f: (layer, 0, f)),
            pl.BlockSpec((None, D_MODEL, TF_HEAD), lambda f: (layer, 0, f)),
            pl.BlockSpec((None, TF_HEAD, D_MODEL), lambda f: (layer, f, 0)),
        ],
        out_specs=[
            pl.BlockSpec((TM_FFN, D_MODEL), lambda f: (0, 0), pipeline_mode=pl.Buffered(1)),
            pl.BlockSpec((D_MODEL, TF_HEAD), lambda f: (0, f)),
            pl.BlockSpec((D_MODEL, TF_HEAD), lambda f: (0, f)),
            pl.BlockSpec((TF_HEAD, D_MODEL), lambda f: (f, 0)),
        ],
        out_shape=[y_shape, wide, wide, tall],
        scratch_shapes=scratch,
        compiler_params=_cparams(("arbitrary",)),
        name="ffn_head",
    )(x, g, w1, w3, w2)

    in_specs = [
        pl.BlockSpec((TM_FFN, D_MODEL), lambda i, f: (i + 1, 0)),
        pl.BlockSpec((None, 1, D_MODEL), lambda i, f: (layer, 0, 0)),
        pl.BlockSpec((D_MODEL, TF), lambda i, f: (0, f)),
        pl.BlockSpec((D_MODEL, TF), lambda i, f: (0, f)),
        pl.BlockSpec((TF, D_MODEL), lambda i, f: (f, 0)),
    ]
    out_specs = [pl.BlockSpec((TM_FFN, D_MODEL), lambda i, f: (i + 1, 0))]
    out_shape = [y_shape]
    cast_inputs = []
    for job in jobs:
        rows = job.arrays[0].shape[1]
        cols = sum(a.shape[2] for a in job.arrays)
        rb, n_sub = _cast_job_blocking(rows, N_CAST_TILES, n_steps)

        def block(i, f, n_sub=n_sub):
            moving = i * n_sub + jnp.minimum(f, n_sub - 1)
            return jnp.where(i < N_CAST_TILES, moving, N_CAST_TILES * n_sub - 1)

        for a in job.arrays:
            in_specs.append(pl.BlockSpec(
                (None, rb, a.shape[2]),
                lambda i, f, block=block, l=job.layer: (l, block(i, f), 0)))
            cast_inputs.append(a)
        out_specs.append(pl.BlockSpec((rb, cols), lambda i, f, block=block: (block(i, f), 0)))
        out_shape.append(jax.ShapeDtypeStruct((rows, cols), BF16))
    in_specs.append(pl.BlockSpec(memory_space=pl.ANY))
    y, *cast = pl.pallas_call(
        functools.partial(_ffn_body_kernel, job_sizes=tuple(len(job.arrays) for job in jobs)),
        grid=(n_tiles - 1, n_steps),
        in_specs=in_specs,
        out_specs=out_specs,
        out_shape=out_shape,
        input_output_aliases={len(in_specs) - 1: 0},
        scratch_shapes=scratch,
        compiler_params=_cparams(("arbitrary", "arbitrary")),
        name="ffn",
    )(x, g, w1b, w3b, w2b, *cast_inputs, y_head)
    return y, cast


def _gmlp_kernel(x_ref, g_ref, win_ref, lng_ref, lnb_ref, ws_ref, bs_ref, wout_ref,
                 o_ref, h_ref, v_ref, s1_ref, s2_ref, mu_ref, rs_ref, gt0_ref, gt1_ref):
    j = pl.program_id(1)
    gt_refs = (gt0_ref, gt1_ref)

    def v_slab(h):
        z = _gelu(_dot(h, win_ref[...]))
        v_ref[j] = z
        p1 = z[:, :LANES]
        p2 = p1 * p1
        for c in range(1, TG // LANES):
            zc = z[:, c * LANES:(c + 1) * LANES]
            p1 += zc
            p2 += zc * zc
        return p1, p2

    @pl.when(j == 0)
    def _():
        xf = x_ref[...]
        o_ref[...] = xf
        h = _rms(xf, g_ref[...]).astype(BF16)
        h_ref[...] = h
        s1_ref[...], s2_ref[...] = v_slab(h)

    @pl.when((j > 0) & (j < N_HALF))
    def _():
        p1, p2 = v_slab(h_ref[...])
        s1_ref[...] += p1
        s2_ref[...] += p2

    def gate(k, dst_ref):
        u = _gelu(_dot(h_ref[...], win_ref[...]))
        if isinstance(k, int) and k == 0:
            mu = jnp.sum(s1_ref[...], axis=-1, keepdims=True) * (1.0 / GMLP_WIDTH)
            ex2 = jnp.sum(s2_ref[...], axis=-1, keepdims=True) * (1.0 / GMLP_WIDTH)
            rs = lax.rsqrt(ex2 - mu * mu + LN_EPS)
            mu_ref[...] = mu
            rs_ref[...] = rs
        else:
            mu, rs = mu_ref[...], rs_ref[...]
        vn = ((v_ref[k] - mu) * rs * lng_ref[k] + lnb_ref[k]).astype(BF16)
        row = lax.broadcasted_iota(jnp.int32, (CHUNK, CHUNK), 0)
        col = lax.broadcasted_iota(jnp.int32, (CHUNK, CHUNK), 1)
        causal = row >= col
        for gg in range(GROUPS_PER_SLAB):
            grp = GROUPS_PER_SLAB * k + gg
            w = jnp.where(causal, ws_ref[grp], 0.0).astype(BF16)
            bias = bs_ref[grp]
            cols = slice(gg * GMLP_GROUP_DIM, (gg + 1) * GMLP_GROUP_DIM)
            for c in range(TM // CHUNK):
                rows = slice(c * CHUNK, (c + 1) * CHUNK)
                s = _dot(w, vn[rows, cols]) + bias
                dst_ref[rows, cols] = (u[rows, cols] * s).astype(BF16)

    def project(src_ref):
        o_ref[...] += _dot(src_ref[...], wout_ref[...])

    @pl.when(j == N_HALF)
    def _():
        gate(0, gt_refs[0])

    for parity in range(2):
        @pl.when((j > N_HALF) & (j < 2 * N_HALF) & (((j - N_HALF) & 1) == parity))
        def _():
            project(gt_refs[1 - parity])
            gate(j - N_HALF, gt_refs[parity])

    @pl.when(j == 2 * N_HALF)
    def _():
        project(gt_refs[(N_HALF - 1) % 2])


def _gmlp(x, g, w_in, ln_g, ln_b, w_s, b_s, w_out, layer, mixer):
    t = x.shape[0]

    def win_map(i, j):
        return (0, jnp.where(j < N_HALF, j + N_HALF, jnp.minimum(j, 2 * N_HALF - 1) - N_HALF))

    def wout_map(i, j):
        return (jnp.where(j <= N_HALF, N_HALF - 1, j - N_HALF - 1), 0)

    def x_map(i, j):
        return (jnp.minimum(i + (j >= 2 * N_HALF - 1).astype(jnp.int32), t // TM - 1), 0)

    return pl.pallas_call(
        _gmlp_kernel,
        grid=(t // TM, 2 * N_HALF + 1),
        in_specs=[
            pl.BlockSpec((TM, D_MODEL), x_map),
            pl.BlockSpec((None, 1, D_MODEL), lambda i, j: (layer, 0, 0)),
            pl.BlockSpec((D_MODEL, TG), win_map),
            pl.BlockSpec((N_HALF, 1, TG), lambda i, j: (0, 0, 0)),
            pl.BlockSpec((N_HALF, 1, TG), lambda i, j: (0, 0, 0)),
            pl.BlockSpec((None, GMLP_GROUPS, CHUNK, CHUNK), lambda i, j: (mixer, 0, 0, 0)),
            pl.BlockSpec((GMLP_GROUPS, CHUNK, GMLP_GROUP_DIM), lambda i, j: (0, 0, 0)),
            pl.BlockSpec((TG, D_MODEL), wout_map),
        ],
        out_specs=pl.BlockSpec((TM, D_MODEL), lambda i, j: (i, 0)),
        out_shape=jax.ShapeDtypeStruct((t, D_MODEL), F32),
        scratch_shapes=[
            pltpu.VMEM((TM, D_MODEL), BF16),
            pltpu.VMEM((N_HALF, TM, TG), F32),
            pltpu.VMEM((TM, LANES), F32),
            pltpu.VMEM((TM, LANES), F32),
            pltpu.VMEM((TM, 1), F32),
            pltpu.VMEM((TM, 1), F32),
            pltpu.VMEM((TM, TG), BF16),
            pltpu.VMEM((TM, TG), BF16),
        ],
        compiler_params=_cparams(("parallel", "arbitrary")),
        name="gmlp",
    )(x, g, w_in, ln_g, ln_b, w_s, b_s, w_out)


def _rope(y, ca, cp, cm):
    n = y.shape[-1]
    reps = n // LANES
    ca = jnp.concatenate([ca] * reps, axis=-1)
    cp = jnp.concatenate([cp] * reps, axis=-1)
    cm = jnp.concatenate([cm] * reps, axis=-1)
    half = ROPE_DIM // 2
    return y * ca + pltpu.roll(y, n - half, 1) * cp + pltpu.roll(y, half, 1) * cm


def _dup_heads(y):
    lane = lax.broadcasted_iota(jnp.int32, (y.shape[0], LANES), 1)
    low = lane < HEAD_DIM
    out = []
    for p in range(y.shape[-1] // LANES):
        blk = y[:, p * LANES:(p + 1) * LANES]
        swapped = pltpu.roll(blk, HEAD_DIM, 1)
        out.append(jnp.where(low, blk, swapped))
        out.append(jnp.where(low, swapped, blk))
    return jnp.concatenate(out, axis=-1)


def _qkv_kernel(x_ref, g_ref, w_ref, b_ref, ca_ref, cp_ref, cm_ref, q_ref, k_ref, v_ref):
    hg, rstd = _rms_split(x_ref[...], g_ref[...])
    ca, cp, cm = ca_ref[...], cp_ref[...], cm_ref[...]
    for n in range(QW // TN):
        cols = slice(n * TN, (n + 1) * TN)
        y = _dot(hg, w_ref[:, cols]) * rstd + b_ref[:, cols]
        q_ref[:, cols] = (_rope(y, ca, cp, cm) * (HEAD_DIM ** -0.5 * LOG2_E)).astype(BF16)
    y = _dot(hg, w_ref[:, QW:]) * rstd + b_ref[:, QW:]
    k_ref[...] = _dup_heads(_rope(y[:, :KW], ca, cp, cm)).astype(BF16)
    v_ref[...] = _dup_heads(y[:, KW:]).astype(BF16)


def _qkv(x, g, w, b, ca, cp, cm, seq, layer, mixer):
    t = x.shape[0]
    wide = QW + 2 * KW
    pos_blocks = seq // TM
    tab = pl.BlockSpec((TM, LANES), lambda i: (i % pos_blocks, 0))
    kv_shape = jax.ShapeDtypeStruct((t, N_KV_HEADS * LANES), BF16)
    return pl.pallas_call(
        _qkv_kernel,
        grid=(t // TM,),
        in_specs=[
            pl.BlockSpec((TM, D_MODEL), lambda i: (i, 0)),
            pl.BlockSpec((None, 1, D_MODEL), lambda i: (layer, 0, 0)),
            pl.BlockSpec((D_MODEL, wide), lambda i: (0, 0)),
            pl.BlockSpec((None, 1, wide), lambda i: (mixer, 0, 0)),
            tab, tab, tab,
        ],
        out_specs=[
            pl.BlockSpec((TM, QW), lambda i: (i, 0)),
            pl.BlockSpec((TM, N_KV_HEADS * LANES), lambda i: (i, 0)),
            pl.BlockSpec((TM, N_KV_HEADS * LANES), lambda i: (i, 0)),
        ],
        out_shape=[jax.ShapeDtypeStruct((t, QW), BF16), kv_shape, kv_shape],
        compiler_params=_cparams(("parallel",)),
        name="qkv",
    )(x, g, w, b, ca, cp, cm)


def _attn_kernel(sink_ref, q_ref, kp_ref, kc_ref, vp_ref, vc_ref, o_ref, kbuf_ref, vbuf_ref, *,
                 tiles_per_seq):
    first_tile = (pl.program_id(0) % tiles_per_seq) == 0
    kbuf_ref[:WINDOW] = kp_ref[...]
    kbuf_ref[WINDOW:] = kc_ref[...]
    vbuf_ref[:WINDOW] = vp_ref[...]
    vbuf_ref[WINDOW:] = vc_ref[...]

    t_idx = lax.broadcasted_iota(jnp.int32, (WINDOW, 2 * WINDOW), 0)
    s_idx = lax.broadcasted_iota(jnp.int32, (WINDOW, 2 * WINDOW), 1)
    diff = t_idx + WINDOW - s_idx
    band = (diff >= 0) & (diff < WINDOW)
    lane = lax.broadcasted_iota(jnp.int32, (WINDOW, LANES), 1)
    low = lane < HEAD_DIM
    zero = jnp.zeros((), BF16)
    sink_col = lax.broadcasted_iota(jnp.int32, (1, 2 * WINDOW), 1) == 0
    sink_row = lax.broadcasted_iota(jnp.int32, (2 * WINDOW, LANES), 0) == 0

    def block(r, carry):
        row0 = pl.multiple_of(r * WINDOW, WINDOW)
        q_rows = pl.ds(row0, WINDOW)
        kv_rows = pl.ds(row0, 2 * WINDOW)
        has_prev = (r > 0) | jnp.logical_not(first_tile)
        valid = band & ((s_idx >= WINDOW) | has_prev)
        for kh in range(N_KV_HEADS):
            ks = slice(kh * LANES, (kh + 1) * LANES)
            k2 = kbuf_ref[kv_rows, ks]
            v2 = jnp.where(sink_row, zero, vbuf_ref[kv_rows, ks])
            q_stack = []
            for hq in range(Q_PER_KV):
                p = kh * (Q_PER_KV // 2) + hq // 2
                qb = q_ref[q_rows, p * LANES:(p + 1) * LANES]
                q_stack.append(jnp.where(low if hq % 2 == 0 else ~low, qb, zero))
            s_all = lax.dot_general(jnp.concatenate(q_stack, axis=0), k2, (((1,), (1,)), ((), ())),
                                    preferred_element_type=F32)
            e_stack, recips = [], []
            for hq in range(Q_PER_KV):
                head = kh * Q_PER_KV + hq
                fill = jnp.where(sink_col, sink_ref[head] * LOG2_E, -jnp.inf)
                s = jnp.where(valid, s_all[hq * WINDOW:(hq + 1) * WINDOW], fill)
                e = jnp.exp2(s - jnp.max(s, axis=-1, keepdims=True))
                recips.append(1.0 / jnp.sum(e, axis=-1, keepdims=True))
                e_stack.append(e.astype(BF16))
            o_all = _dot(jnp.concatenate(e_stack, axis=0), v2)
            for pp in range(Q_PER_KV // 2):
                p = kh * (Q_PER_KV // 2) + pp
                halves = [o_all[hq * WINDOW:(hq + 1) * WINDOW] * recips[hq] for hq in (2 * pp, 2 * pp + 1)]
                o_ref[q_rows, p * LANES:(p + 1) * LANES] = (
                    jnp.where(low, halves[0], halves[1]).astype(BF16))
        return carry

    lax.fori_loop(0, TM // WINDOW, block, 0)


def _attn(sinks, q, k2, v2, seq):
    t = q.shape[0]
    kvw = N_KV_HEADS * LANES
    blocks_per_tile = TM // WINDOW
    cur = lambda i: (i, 0)
    prev = lambda i: (jnp.maximum(i * blocks_per_tile - 1, 0), 0)
    return pl.pallas_call(
        functools.partial(_attn_kernel, tiles_per_seq=seq // TM),
        grid=(t // TM,),
        in_specs=[
            pl.BlockSpec(memory_space=pltpu.SMEM),
            pl.BlockSpec((TM, QW), cur),
            pl.BlockSpec((WINDOW, kvw), prev),
            pl.BlockSpec((TM, kvw), cur),
            pl.BlockSpec((WINDOW, kvw), prev),
            pl.BlockSpec((TM, kvw), cur),
        ],
        out_specs=pl.BlockSpec((TM, QW), cur),
        out_shape=jax.ShapeDtypeStruct((t, QW), BF16),
        scratch_shapes=[pltpu.VMEM((WINDOW + TM, kvw), BF16), pltpu.VMEM((WINDOW + TM, kvw), BF16)],
        compiler_params=_cparams(("parallel",)),
        name="attn",
    )(sinks, q, k2, k2, v2, v2)


def _wo_kernel(x_ref, a_ref, w_ref, b_ref, o_ref):
    o_ref[...] = x_ref[...] + _dot(a_ref[...], w_ref[...]) + b_ref[...]


def _wo(x, a, w, b, mixer):
    t = x.shape[0]
    return pl.pallas_call(
        _wo_kernel,
        grid=(t // TM,),
        in_specs=[
            pl.BlockSpec((TM, D_MODEL), lambda i: (i, 0)),
            pl.BlockSpec((TM, QW), lambda i: (i, 0)),
            pl.BlockSpec((QW, D_MODEL), lambda i: (0, 0)),
            pl.BlockSpec((None, 1, D_MODEL), lambda i: (mixer, 0, 0)),
        ],
        out_specs=pl.BlockSpec((TM, D_MODEL), lambda i: (i, 0)),
        out_shape=jax.ShapeDtypeStruct((t, D_MODEL), F32),
        compiler_params=_cparams(("parallel",)),
        name="wo",
    )(x, a, w, b)


def _ple_kernel(x_ref, p_ref, g_ref, wg_ref, wp_ref, fg_ref, o_ref, *, final):
    xf = x_ref[...]
    hg, rstd = _rms_split(xf, g_ref[...])
    gate = jax.nn.sigmoid(_dot(hg, wg_ref[...]) * rstd)
    proj = _dot(p_ref[...].astype(BF16), wp_ref[...])
    y = xf + gate * proj
    if final:
        y = _rms(y, fg_ref[...])
    o_ref[...] = y


def _ple(x, p, g, wg, wp, fg, layer, final):
    t = x.shape[0]
    return pl.pallas_call(
        functools.partial(_ple_kernel, final=final),
        grid=(t // TM,),
        in_specs=[
            pl.BlockSpec((TM, D_MODEL), lambda i: (i, 0)),
            pl.BlockSpec((None, TM, PLE_DIM), lambda i: (layer, i, 0)),
            pl.BlockSpec((None, 1, D_MODEL), lambda i: (layer, 0, 0)),
            pl.BlockSpec((D_MODEL, D_MODEL), lambda i: (0, 0)),
            pl.BlockSpec((PLE_DIM, D_MODEL), lambda i: (0, 0)),
            pl.BlockSpec((1, D_MODEL), lambda i: (0, 0)),
        ],
        out_specs=pl.BlockSpec((TM, D_MODEL), lambda i: (i, 0)),
        out_shape=jax.ShapeDtypeStruct((t, D_MODEL), F32),
        compiler_params=_cparams(("parallel",)),
        name="ple_final" if final else "ple",
    )(x, p, g, wg, wp, fg)


def _rope_coefficients(seq):
    half = ROPE_DIM // 2
    inv_freq = np.float32(ROPE_THETA) ** (-np.arange(0, ROPE_DIM, 2, dtype=np.float32) / np.float32(ROPE_DIM))
    ang = (np.arange(seq, dtype=np.float32)[:, None] * inv_freq[None, :]).astype(np.float32)
    cos = np.cos(ang.astype(np.float64)).astype(np.float32)
    sin = np.sin(ang.astype(np.float64)).astype(np.float32)
    ones = np.ones((seq, HEAD_DIM - ROPE_DIM), np.float32)
    zeros = np.zeros((seq, HEAD_DIM - half), np.float32)
    ca = np.concatenate([cos, cos, ones], axis=-1)
    cp = np.concatenate([-sin, zeros], axis=-1)
    cm = np.concatenate([np.zeros((seq, half), np.float32), sin, zeros[:, half:]], axis=-1)
    tile = lambda a: jnp.asarray(np.concatenate([a] * (LANES // HEAD_DIM), axis=-1))
    return tile(ca), tile(cp), tile(cm)


def kernel(x, p, ffn1_norm, ffn1_w1, ffn1_w3, ffn1_w2, mix_norm, ffn2_norm, ffn2_w1, ffn2_w3, ffn2_w2, ple_norm, ple_w_gate, ple_w_proj, gmlp_w_in, gmlp_ln_g, gmlp_ln_b, gmlp_w_s, gmlp_b_s, gmlp_w_out, swa_wq, swa_bq, swa_wk, swa_bk, swa_wv, swa_bv, swa_sinks, swa_wo, swa_bo, final_norm):
    batch, seq, _ = x.shape
    t = batch * seq
    assert seq % TM == 0 and TM % WINDOW == 0 and t % TM_FFN == 0
    rows = lambda a: a.reshape(a.shape[0], 1, a.shape[1])
    xs = x.reshape(t, D_MODEL)
    ps = p.reshape(DEPTH, t, PLE_DIM)
    ca, cp, cm = _rope_coefficients(seq)

    ffn1_g, ffn2_g, mix_g, ple_g = rows(ffn1_norm), rows(ffn2_norm), rows(mix_norm), rows(ple_norm)
    swa_bqkv = rows(jnp.concatenate([swa_bq, swa_bk, swa_bv], axis=1))
    fg = final_norm.reshape(1, D_MODEL)

    for i in range(DEPTH):
        j = i // 2
        if i % 2 == 0:
            mixer_jobs = [_CastJob(j, (gmlp_w_in,)), _CastJob(j, (gmlp_w_out,))]
        else:
            mixer_jobs = [_CastJob(j, (swa_wq, swa_wk, swa_wv)), _CastJob(j, (swa_wo,))]
        xs, (mix_w0, mix_w1) = _ffn(xs, ffn1_g, ffn1_w1, ffn1_w3, ffn1_w2, layer=i, jobs=mixer_jobs)
        if i % 2 == 0:
            b_s = jnp.broadcast_to(gmlp_b_s[j][:, :, None], (GMLP_GROUPS, CHUNK, GMLP_GROUP_DIM))
            xs = _gmlp(xs, mix_g, mix_w0,
                       gmlp_ln_g[j].reshape(N_HALF, 1, TG), gmlp_ln_b[j].reshape(N_HALF, 1, TG),
                       gmlp_w_s, b_s, mix_w1, layer=i, mixer=j)
        else:
            q, k2, v2 = _qkv(xs, mix_g, mix_w0, swa_bqkv, ca, cp, cm, seq, layer=i, mixer=j)
            a = _attn(swa_sinks[j], q, k2, v2, seq)
            xs = _wo(xs, a, mix_w1, rows(swa_bo), mixer=j)
        ple_jobs = [_CastJob(i, (ple_w_gate,)), _CastJob(i, (ple_w_proj,))]
        xs, (ple_wg, ple_wp) = _ffn(xs, ffn2_g, ffn2_w1, ffn2_w3, ffn2_w2, layer=i, jobs=ple_jobs)
        xs = _ple(xs, ps, ple_g, ple_wg, ple_wp, fg, layer=i, final=(i == DEPTH - 1))
    return xs.reshape(batch, seq, D_MODEL)
```

```python
import functools
import math
from typing import NamedTuple

import jax
import jax.numpy as jnp
import numpy as np
from jax import lax
from jax.experimental import pallas as pl
from jax.experimental.pallas import tpu as pltpu

F32 = jnp.float32
BF16 = jnp.bfloat16

D_MODEL = 2048
DEPTH = 2
D_FF = 5632
PLE_DIM = 256
RMS_EPS = 1e-6
LN_EPS = 1e-5
CHUNK = 128
GMLP_WIDTH = 2 * D_MODEL
GMLP_GROUPS = 16
GMLP_GROUP_DIM = GMLP_WIDTH // GMLP_GROUPS
N_Q_HEADS = 32
N_KV_HEADS = 4
HEAD_DIM = 64
Q_PER_KV = N_Q_HEADS // N_KV_HEADS
WINDOW = 128
ROPE_THETA = 500000.0
ROPE_DIM = HEAD_DIM // 4
LOG2_E = 1.4426950408889634
QW = N_Q_HEADS * HEAD_DIM
KW = N_KV_HEADS * HEAD_DIM

LANES = 128
BF16_SUBLANES = 16

TM = 512
TM_FFN = 1024
TF = 512
TN = 512
TG = 1024
N_HALF = GMLP_WIDTH // TG
GROUPS_PER_SLAB = TG // GMLP_GROUP_DIM
CAST_BLOCK_BYTES = 6 * 1024 * 1024
VMEM_LIMIT = 62 * 1024 * 1024


def _cparams(sem):
    return pltpu.CompilerParams(dimension_semantics=sem, vmem_limit_bytes=VMEM_LIMIT)


def _dot(a, b):
    return jnp.dot(a, b, preferred_element_type=F32)


def _rms(xf, g):
    ms = jnp.mean(xf * xf, axis=-1, keepdims=True)
    return xf * lax.rsqrt(ms + RMS_EPS) * g


def _rms_split(xf, g):
    ms = jnp.mean(xf * xf, axis=-1, keepdims=True)
    return (xf * g).astype(BF16), lax.rsqrt(ms + RMS_EPS)


def _gelu(z):
    return 0.5 * z * (1.0 + lax.erf(z * (1.0 / math.sqrt(2.0))))


def _cast_kernel(*refs):
    *in_refs, o_ref = refs
    col = 0
    for r in in_refs:
        w = r.shape[-1]
        o_ref[:, col:col + w] = r[...].astype(BF16)
        col += w


def _cast_bf16(layer, *ws):
    rows = ws[0].shape[1]
    cols = sum(w.shape[2] for w in ws)
    rb = rows
    while rb * cols * 4 > CAST_BLOCK_BYTES and rb % 32 == 0:
        rb //= 2
    return pl.pallas_call(
        _cast_kernel,
        grid=(rows // rb,),
        in_specs=[pl.BlockSpec((None, rb, w.shape[2]), lambda r: (layer, r, 0)) for w in ws],
        out_specs=pl.BlockSpec((rb, cols), lambda r: (r, 0)),
        out_shape=jax.ShapeDtypeStruct((rows, cols), BF16),
        compiler_params=_cparams(("parallel",)),
        name="cast",
    )(*ws)


class _CastJob(NamedTuple):
    layer: int
    arrays: tuple


def _cast_job_blocking(rows, n_tiles, n_steps):
    per_tile = rows // n_tiles
    for n_sub in range(n_steps, 0, -1):
        if per_tile % n_sub == 0 and (per_tile // n_sub) % BF16_SUBLANES == 0:
            return per_tile // n_sub, n_sub
    raise ValueError(f"cannot spread a {rows}-row cast over {n_tiles} tiles")


def _ffn_kernel(*refs, job_sizes):
    n_cast_in = sum(job_sizes)
    x_ref, g_ref, w1_ref, w3_ref, w2_ref = refs[:5]
    cast_in = refs[5:5 + n_cast_in]
    o_ref = refs[5 + n_cast_in]
    cast_out = refs[6 + n_cast_in:6 + n_cast_in + len(job_sizes)]
    h_ref, rs_ref = refs[-2:]
    f = pl.program_id(1)

    def swiglu(hg, rstd):
        a = _dot(hg, w1_ref[...]) * rstd
        b = _dot(hg, w3_ref[...]) * rstd
        gate = (0.5 * (a * jax.nn.sigmoid(a)) * b).astype(BF16)
        return _dot(gate, w2_ref[...])

    def side_casts():
        start = 0
        for n, out_ref in zip(job_sizes, cast_out):
            _cast_kernel(*cast_in[start:start + n], out_ref)
            start += n

    @pl.when(f == 0)
    def _():
        xf = x_ref[...]
        hg, rstd = _rms_split(xf, g_ref[...])
        h_ref[...] = hg
        rs_ref[...] = rstd
        o_ref[...] = xf + swiglu(hg, rstd)
        side_casts()

    @pl.when(f > 0)
    def _():
        o_ref[...] += swiglu(h_ref[...], rs_ref[...])
        side_casts()


def _ffn(x, g, w1, w3, w2, layer, jobs):
    t = x.shape[0]
    n_tiles, n_steps = t // TM_FFN, D_FF // TF
    in_specs = [
        pl.BlockSpec((TM_FFN, D_MODEL), lambda i, f: (i, 0)),
        pl.BlockSpec((None, 1, D_MODEL), lambda i, f: (layer, 0, 0)),
        pl.BlockSpec((D_MODEL, TF), lambda i, f: (0, f)),
        pl.BlockSpec((D_MODEL, TF), lambda i, f: (0, f)),
        pl.BlockSpec((TF, D_MODEL), lambda i, f: (f, 0)),
    ]
    out_specs = [pl.BlockSpec((TM_FFN, D_MODEL), lambda i, f: (i, 0))]
    out_shape = [jax.ShapeDtypeStruct((t, D_MODEL), F32)]
    cast_inputs = []
    for job in jobs:
        rows = job.arrays[0].shape[1]
        cols = sum(a.shape[2] for a in job.arrays)
        rb, n_sub = _cast_job_blocking(rows, n_tiles, n_steps)
        block = lambda i, f, n_sub=n_sub: i * n_sub + jnp.minimum(f, n_sub - 1)
        for a in job.arrays:
            in_specs.append(pl.BlockSpec(
                (None, rb, a.shape[2]),
                lambda i, f, block=block, l=job.layer: (l, block(i, f), 0)))
            cast_inputs.append(a)
        out_specs.append(pl.BlockSpec((rb, cols), lambda i, f, block=block: (block(i, f), 0)))
        out_shape.append(jax.ShapeDtypeStruct((rows, cols), BF16))
    y, *cast = pl.pallas_call(
        functools.partial(_ffn_kernel, job_sizes=tuple(len(job.arrays) for job in jobs)),
        grid=(n_tiles, n_steps),
        in_specs=in_specs,
        out_specs=out_specs,
        out_shape=out_shape,
        scratch_shapes=[pltpu.VMEM((TM_FFN, D_MODEL), BF16), pltpu.VMEM((TM_FFN, 1), F32)],
        compiler_params=_cparams(("arbitrary", "arbitrary")),
        name="ffn",
    )(x, g, w1, w3, w2, *cast_inputs)
    return y, cast


def _gmlp_kernel(x_ref, g_ref, win_ref, lng_ref, lnb_ref, ws_ref, bs_ref, wout_ref,
                 o_ref, h_ref, v_ref, s1_ref, s2_ref, mu_ref, rs_ref, gt0_ref, gt1_ref):
    j = pl.program_id(1)
    gt_refs = (gt0_ref, gt1_ref)

    def v_slab(h):
        z = _gelu(_dot(h, win_ref[...]))
        v_ref[j] = z
        p1 = z[:, :LANES]
        p2 = p1 * p1
        for c in range(1, TG // LANES):
            zc = z[:, c * LANES:(c + 1) * LANES]
            p1 += zc
            p2 += zc * zc
        return p1, p2

    @pl.when(j == 0)
    def _():
        xf = x_ref[...]
        o_ref[...] = xf
        h = _rms(xf, g_ref[...]).astype(BF16)
        h_ref[...] = h
        s1_ref[...], s2_ref[...] = v_slab(h)

    @pl.when((j > 0) & (j < N_HALF))
    def _():
        p1, p2 = v_slab(h_ref[...])
        s1_ref[...] += p1
        s2_ref[...] += p2

    def gate(k, dst_ref):
        u = _gelu(_dot(h_ref[...], win_ref[...]))
        if isinstance(k, int) and k == 0:
            mu = jnp.sum(s1_ref[...], axis=-1, keepdims=True) * (1.0 / GMLP_WIDTH)
            ex2 = jnp.sum(s2_ref[...], axis=-1, keepdims=True) * (1.0 / GMLP_WIDTH)
            rs = lax.rsqrt(ex2 - mu * mu + LN_EPS)
            mu_ref[...] = mu
            rs_ref[...] = rs
        else:
            mu, rs = mu_ref[...], rs_ref[...]
        vn = ((v_ref[k] - mu) * rs * lng_ref[k] + lnb_ref[k]).astype(BF16)
        row = lax.broadcasted_iota(jnp.int32, (CHUNK, CHUNK), 0)
        col = lax.broadcasted_iota(jnp.int32, (CHUNK, CHUNK), 1)
        causal = row >= col
        for gg in range(GROUPS_PER_SLAB):
            grp = GROUPS_PER_SLAB * k + gg
            w = jnp.where(causal, ws_ref[grp], 0.0).astype(BF16)
            bias = bs_ref[grp]
            cols = slice(gg * GMLP_GROUP_DIM, (gg + 1) * GMLP_GROUP_DIM)
            for c in range(TM // CHUNK):
                rows = slice(c * CHUNK, (c + 1) * CHUNK)
                s = _dot(w, vn[rows, cols]) + bias
                dst_ref[rows, cols] = (u[rows, cols] * s).astype(BF16)

    def project(src_ref):
        o_ref[...] += _dot(src_ref[...], wout_ref[...])

    @pl.when(j == N_HALF)
    def _():
        gate(0, gt_refs[0])

    for parity in range(2):
        @pl.when((j > N_HALF) & (j < 2 * N_HALF) & (((j - N_HALF) & 1) == parity))
        def _():
            project(gt_refs[1 - parity])
            gate(j - N_HALF, gt_refs[parity])

    @pl.when(j == 2 * N_HALF)
    def _():
        project(gt_refs[(N_HALF - 1) % 2])


def _gmlp(x, g, w_in, ln_g, ln_b, w_s, b_s, w_out, layer, mixer):
    t = x.shape[0]

    def win_map(i, j):
        return (0, jnp.where(j < N_HALF, j + N_HALF, jnp.minimum(j, 2 * N_HALF - 1) - N_HALF))

    def wout_map(i, j):
        return (jnp.where(j <= N_HALF, N_HALF - 1, j - N_HALF - 1), 0)

    def x_map(i, j):
        return (jnp.minimum(i + (j >= 2 * N_HALF - 1).astype(jnp.int32), t // TM - 1), 0)

    return pl.pallas_call(
        _gmlp_kernel,
        grid=(t // TM, 2 * N_HALF + 1),
        in_specs=[
            pl.BlockSpec((TM, D_MODEL), x_map),
            pl.BlockSpec((None, 1, D_MODEL), lambda i, j: (layer, 0, 0)),
            pl.BlockSpec((D_MODEL, TG), win_map),
            pl.BlockSpec((N_HALF, 1, TG), lambda i, j: (0, 0, 0)),
            pl.BlockSpec((N_HALF, 1, TG), lambda i, j: (0, 0, 0)),
            pl.BlockSpec((None, GMLP_GROUPS, CHUNK, CHUNK), lambda i, j: (mixer, 0, 0, 0)),
            pl.BlockSpec((GMLP_GROUPS, CHUNK, GMLP_GROUP_DIM), lambda i, j: (0, 0, 0)),
            pl.BlockSpec((TG, D_MODEL), wout_map),
        ],
        out_specs=pl.BlockSpec((TM, D_MODEL), lambda i, j: (i, 0)),
        out_shape=jax.ShapeDtypeStruct((t, D_MODEL), F32),
        scratch_shapes=[
            pltpu.VMEM((TM, D_MODEL), BF16),
            pltpu.VMEM((N_HALF, TM, TG), F32),
            pltpu.VMEM((TM, LANES), F32),
            pltpu.VMEM((TM, LANES), F32),
            pltpu.VMEM((TM, 1), F32),
            pltpu.VMEM((TM, 1), F32),
            pltpu.VMEM((TM, TG), BF16),
            pltpu.VMEM((TM, TG), BF16),
        ],
        compiler_params=_cparams(("parallel", "arbitrary")),
        name="gmlp",
    )(x, g, w_in, ln_g, ln_b, w_s, b_s, w_out)


def _rope(y, ca, cp, cm):
    n = y.shape[-1]
    reps = n // LANES
    ca = jnp.concatenate([ca] * reps, axis=-1)
    cp = jnp.concatenate([cp] * reps, axis=-1)
    cm = jnp.concatenate([cm] * reps, axis=-1)
    half = ROPE_DIM // 2
    return y * ca + pltpu.roll(y, n - half, 1) * cp + pltpu.roll(y, half, 1) * cm


def _dup_heads(y):
    lane = lax.broadcasted_iota(jnp.int32, (y.shape[0], LANES), 1)
    low = lane < HEAD_DIM
    out = []
    for p in range(y.shape[-1] // LANES):
        blk = y[:, p * LANES:(p + 1) * LANES]
        swapped = pltpu.roll(blk, HEAD_DIM, 1)
        out.append(jnp.where(low, blk, swapped))
        out.append(jnp.where(low, swapped, blk))
    return jnp.concatenate(out, axis=-1)


def _qkv_kernel(x_ref, g_ref, w_ref, b_ref, ca_ref, cp_ref, cm_ref, q_ref, k_ref, v_ref):
    hg, rstd = _rms_split(x_ref[...], g_ref[...])
    ca, cp, cm = ca_ref[...], cp_ref[...], cm_ref[...]
    for n in range(QW // TN):
        cols = slice(n * TN, (n + 1) * TN)
        y = _dot(hg, w_ref[:, cols]) * rstd + b_ref[:, cols]
        q_ref[:, cols] = (_rope(y, ca, cp, cm) * (HEAD_DIM ** -0.5 * LOG2_E)).astype(BF16)
    y = _dot(hg, w_ref[:, QW:]) * rstd + b_ref[:, QW:]
    k_ref[...] = _dup_heads(_rope(y[:, :KW], ca, cp, cm)).astype(BF16)
    v_ref[...] = _dup_heads(y[:, KW:]).astype(BF16)


def _qkv(x, g, w, b, ca, cp, cm, seq, layer, mixer):
    t = x.shape[0]
    wide = QW + 2 * KW
    pos_blocks = seq // TM
    tab = pl.BlockSpec((TM, LANES), lambda i: (i % pos_blocks, 0))
    kv_shape = jax.ShapeDtypeStruct((t, N_KV_HEADS * LANES), BF16)
    return pl.pallas_call(
        _qkv_kernel,
        grid=(t // TM,),
        in_specs=[
            pl.BlockSpec((TM, D_MODEL), lambda i: (i, 0)),
            pl.BlockSpec((None, 1, D_MODEL), lambda i: (layer, 0, 0)),
            pl.BlockSpec((D_MODEL, wide), lambda i: (0, 0)),
            pl.BlockSpec((None, 1, wide), lambda i: (mixer, 0, 0)),
            tab, tab, tab,
        ],
        out_specs=[
            pl.BlockSpec((TM, QW), lambda i: (i, 0)),
            pl.BlockSpec((TM, N_KV_HEADS * LANES), lambda i: (i, 0)),
            pl.BlockSpec((TM, N_KV_HEADS * LANES), lambda i: (i, 0)),
        ],
        out_shape=[jax.ShapeDtypeStruct((t, QW), BF16), kv_shape, kv_shape],
        compiler_params=_cparams(("parallel",)),
        name="qkv",
    )(x, g, w, b, ca, cp, cm)


def _attn_kernel(sink_ref, q_ref, kp_ref, kc_ref, vp_ref, vc_ref, o_ref, kbuf_ref, vbuf_ref, *,
                 tiles_per_seq):
    first_tile = (pl.program_id(0) % tiles_per_seq) == 0
    kbuf_ref[:WINDOW] = kp_ref[...]
    kbuf_ref[WINDOW:] = kc_ref[...]
    vbuf_ref[:WINDOW] = vp_ref[...]
    vbuf_ref[WINDOW:] = vc_ref[...]

    t_idx = lax.broadcasted_iota(jnp.int32, (WINDOW, 2 * WINDOW), 0)
    s_idx = lax.broadcasted_iota(jnp.int32, (WINDOW, 2 * WINDOW), 1)
    diff = t_idx + WINDOW - s_idx
    band = (diff >= 0) & (diff < WINDOW)
    lane = lax.broadcasted_iota(jnp.int32, (WINDOW, LANES), 1)
    low = lane < HEAD_DIM
    zero = jnp.zeros((), BF16)
    sink_col = lax.broadcasted_iota(jnp.int32, (1, 2 * WINDOW), 1) == 0
    sink_row = lax.broadcasted_iota(jnp.int32, (2 * WINDOW, LANES), 0) == 0

    def block(r, carry):
        row0 = pl.multiple_of(r * WINDOW, WINDOW)
        q_rows = pl.ds(row0, WINDOW)
        kv_rows = pl.ds(row0, 2 * WINDOW)
        has_prev = (r > 0) | jnp.logical_not(first_tile)
        valid = band & ((s_idx >= WINDOW) | has_prev)
        for kh in range(N_KV_HEADS):
            ks = slice(kh * LANES, (kh + 1) * LANES)
            k2 = kbuf_ref[kv_rows, ks]
            v2 = jnp.where(sink_row, zero, vbuf_ref[kv_rows, ks])
            q_stack = []
            for hq in range(Q_PER_KV):
                p = kh * (Q_PER_KV // 2) + hq // 2
                qb = q_ref[q_rows, p * LANES:(p + 1) * LANES]
                q_stack.append(jnp.where(low if hq % 2 == 0 else ~low, qb, zero))
            s_all = lax.dot_general(jnp.concatenate(q_stack, axis=0), k2, (((1,), (1,)), ((), ())),
                                    preferred_element_type=F32)
            e_stack, recips = [], []
            for hq in range(Q_PER_KV):
                head = kh * Q_PER_KV + hq
                fill = jnp.where(sink_col, sink_ref[head] * LOG2_E, -jnp.inf)
                s = jnp.where(valid, s_all[hq * WINDOW:(hq + 1) * WINDOW], fill)
                e = jnp.exp2(s - jnp.max(s, axis=-1, keepdims=True))
                recips.append(1.0 / jnp.sum(e, axis=-1, keepdims=True))
                e_stack.append(e.astype(BF16))
            o_all = _dot(jnp.concatenate(e_stack, axis=0), v2)
            for pp in range(Q_PER_KV // 2):
                p = kh * (Q_PER_KV // 2) + pp
                halves = [o_all[hq * WINDOW:(hq + 1) * WINDOW] * recips[hq] for hq in (2 * pp, 2 * pp + 1)]
                o_ref[q_rows, p * LANES:(p + 1) * LANES] = (
                    jnp.where(low, halves[0], halves[1]).astype(BF16))
        return carry

    lax.fori_loop(0, TM // WINDOW, block, 0)


def _attn(sinks, q, k2, v2, seq):
    t = q.shape[0]
    kvw = N_KV_HEADS * LANES
    blocks_per_tile = TM // WINDOW
    cur = lambda i: (i, 0)
    prev = lambda i: (jnp.maximum(i * blocks_per_tile - 1, 0), 0)
    return pl.pallas_call(
        functools.partial(_attn_kernel, tiles_per_seq=seq // TM),
        grid=(t // TM,),
        in_specs=[
            pl.BlockSpec(memory_space=pltpu.SMEM),
            pl.BlockSpec((TM, QW), cur),
            pl.BlockSpec((WINDOW, kvw), prev),
            pl.BlockSpec((TM, kvw), cur),
            pl.BlockSpec((WINDOW, kvw), prev),
            pl.BlockSpec((TM, kvw), cur),
        ],
        out_specs=pl.BlockSpec((TM, QW), cur),
        out_shape=jax.ShapeDtypeStruct((t, QW), BF16),
        scratch_shapes=[pltpu.VMEM((WINDOW + TM, kvw), BF16), pltpu.VMEM((WINDOW + TM, kvw), BF16)],
        compiler_params=_cparams(("parallel",)),
        name="attn",
    )(sinks, q, k2, k2, v2, v2)


def _wo_kernel(x_ref, a_ref, w_ref, b_ref, o_ref):
    o_ref[...] = x_ref[...] + _dot(a_ref[...], w_ref[...]) + b_ref[...]


def _wo(x, a, w, b, mixer):
    t = x.shape[0]
    return pl.pallas_call(
        _wo_kernel,
        grid=(t // TM,),
        in_specs=[
            pl.BlockSpec((TM, D_MODEL), lambda i: (i, 0)),
            pl.BlockSpec((TM, QW), lambda i: (i, 0)),
            pl.BlockSpec((QW, D_MODEL), lambda i: (0, 0)),
            pl.BlockSpec((None, 1, D_MODEL), lambda i: (mixer, 0, 0)),
        ],
        out_specs=pl.BlockSpec((TM, D_MODEL), lambda i: (i, 0)),
        out_shape=jax.ShapeDtypeStruct((t, D_MODEL), F32),
        compiler_params=_cparams(("parallel",)),
        name="wo",
    )(x, a, w, b)


def _ple_kernel(x_ref, p_ref, g_ref, wg_ref, wp_ref, fg_ref, o_ref, *, final):
    xf = x_ref[...]
    hg, rstd = _rms_split(xf, g_ref[...])
    gate = jax.nn.sigmoid(_dot(hg, wg_ref[...]) * rstd)
    proj = _dot(p_ref[...].astype(BF16), wp_ref[...])
    y = xf + gate * proj
    if final:
        y = _rms(y, fg_ref[...])
    o_ref[...] = y


def _ple(x, p, g, wg, wp, fg, layer, final):
    t = x.shape[0]
    return pl.pallas_call(
        functools.partial(_ple_kernel, final=final),
        grid=(t // TM,),
        in_specs=[
            pl.BlockSpec((TM, D_MODEL), lambda i: (i, 0)),
            pl.BlockSpec((None, TM, PLE_DIM), lambda i: (layer, i, 0)),
            pl.BlockSpec((None, 1, D_MODEL), lambda i: (layer, 0, 0)),
            pl.BlockSpec((D_MODEL, D_MODEL), lambda i: (0, 0)),
            pl.BlockSpec((PLE_DIM, D_MODEL), lambda i: (0, 0)),
            pl.BlockSpec((1, D_MODEL), lambda i: (0, 0)),
        ],
        out_specs=pl.BlockSpec((TM, D_MODEL), lambda i: (i, 0)),
        out_shape=jax.ShapeDtypeStruct((t, D_MODEL), F32),
        compiler_params=_cparams(("parallel",)),
        name="ple_final" if final else "ple",
    )(x, p, g, wg, wp, fg)


def _rope_coefficients(seq):
    half = ROPE_DIM // 2
    inv_freq = np.float32(ROPE_THETA) ** (-np.arange(0, ROPE_DIM, 2, dtype=np.float32) / np.float32(ROPE_DIM))
    ang = (np.arange(seq, dtype=np.float32)[:, None] * inv_freq[None, :]).astype(np.float32)
    cos = np.cos(ang.astype(np.float64)).astype(np.float32)
    sin = np.sin(ang.astype(np.float64)).astype(np.float32)
    ones = np.ones((seq, HEAD_DIM - ROPE_DIM), np.float32)
    zeros = np.zeros((seq, HEAD_DIM - half), np.float32)
    ca = np.concatenate([cos, cos, ones], axis=-1)
    cp = np.concatenate([-sin, zeros], axis=-1)
    cm = np.concatenate([np.zeros((seq, half), np.float32), sin, zeros[:, half:]], axis=-1)
    tile = lambda a: jnp.asarray(np.concatenate([a] * (LANES // HEAD_DIM), axis=-1))
    return tile(ca), tile(cp), tile(cm)


def kernel(x, p, ffn1_norm, ffn1_w1, ffn1_w3, ffn1_w2, mix_norm, ffn2_norm, ffn2_w1, ffn2_w3, ffn2_w2, ple_norm, ple_w_gate, ple_w_proj, gmlp_w_in, gmlp_ln_g, gmlp_ln_b, gmlp_w_s, gmlp_b_s, gmlp_w_out, swa_wq, swa_bq, swa_wk, swa_bk, swa_wv, swa_bv, swa_sinks, swa_wo, swa_bo, final_norm):
    batch, seq, _ = x.shape
    t = batch * seq
    assert seq % TM == 0 and TM % WINDOW == 0 and t % TM_FFN == 0
    rows = lambda a: a.reshape(a.shape[0], 1, a.shape[1])
    xs = x.reshape(t, D_MODEL)
    ps = p.reshape(DEPTH, t, PLE_DIM)
    ca, cp, cm = _rope_coefficients(seq)

    ffn1_g, ffn2_g, mix_g, ple_g = rows(ffn1_norm), rows(ffn2_norm), rows(mix_norm), rows(ple_norm)
    swa_bqkv = rows(jnp.concatenate([swa_bq, swa_bk, swa_bv], axis=1))
    fg = final_norm.reshape(1, D_MODEL)

    ffn_w = [_cast_bf16(0, w) for w in (ffn1_w1, ffn1_w3, ffn1_w2)]
    for i in range(DEPTH):
        j = i // 2
        if i % 2 == 0:
            mixer_jobs = [_CastJob(j, (gmlp_w_in,)), _CastJob(j, (gmlp_w_out,))]
        else:
            mixer_jobs = [_CastJob(j, (swa_wq, swa_wk, swa_wv)), _CastJob(j, (swa_wo,))]
        ffn2_jobs = [_CastJob(i, (w,)) for w in (ffn2_w1, ffn2_w3, ffn2_w2)]
        xs, (mix_w0, mix_w1, *ffn_w) = _ffn(xs, ffn1_g, *ffn_w, layer=i, jobs=mixer_jobs + ffn2_jobs)
        if i % 2 == 0:
            b_s = jnp.broadcast_to(gmlp_b_s[j][:, :, None], (GMLP_GROUPS, CHUNK, GMLP_GROUP_DIM))
            xs = _gmlp(xs, mix_g, mix_w0,
                       gmlp_ln_g[j].reshape(N_HALF, 1, TG), gmlp_ln_b[j].reshape(N_HALF, 1, TG),
                       gmlp_w_s, b_s, mix_w1, layer=i, mixer=j)
        else:
            q, k2, v2 = _qkv(xs, mix_g, mix_w0, swa_bqkv, ca, cp, cm, seq, layer=i, mixer=j)
            a = _attn(swa_sinks[j], q, k2, v2, seq)
            xs = _wo(xs, a, mix_w1, rows(swa_bo), mixer=j)
        ple_jobs = [_CastJob(i, (ple_w_gate,)), _CastJob(i, (ple_w_proj,))]
        next_jobs = [_CastJob(i + 1, (w,)) for w in (ffn1_w1, ffn1_w3, ffn1_w2)] if i + 1 < DEPTH else []
        xs, (ple_wg, ple_wp, *ffn_w) = _ffn(xs, ffn2_g, *ffn_w, layer=i, jobs=ple_jobs + next_jobs)
        xs = _ple(xs, ps, ple_g, ple_wg, ple_wp, fg, layer=i, final=(i == DEPTH - 1))
    return xs.reshape(batch, seq, D_MODEL)
```

```python
import functools
import math
from typing import NamedTuple

import jax
import jax.numpy as jnp
import numpy as np
from jax import lax
from jax.experimental import pallas as pl
from jax.experimental.pallas import tpu as pltpu

F32 = jnp.float32
BF16 = jnp.bfloat16

D_MODEL = 2048
DEPTH = 2
D_FF = 5632
PLE_DIM = 256
RMS_EPS = 1e-6
LN_EPS = 1e-5
CHUNK = 128
GMLP_WIDTH = 2 * D_MODEL
GMLP_GROUPS = 16
GMLP_GROUP_DIM = GMLP_WIDTH // GMLP_GROUPS
N_Q_HEADS = 32
N_KV_HEADS = 4
HEAD_DIM = 64
Q_PER_KV = N_Q_HEADS // N_KV_HEADS
WINDOW = 128
ROPE_THETA = 500000.0
ROPE_DIM = HEAD_DIM // 4
LOG2_E = 1.4426950408889634
QW = N_Q_HEADS * HEAD_DIM
KW = N_KV_HEADS * HEAD_DIM

LANES = 128
BF16_SUBLANES = 16

TM = 512
TM_FFN = 1024
TF = 512
TN = 512
TG = 1024
N_HALF = GMLP_WIDTH // TG
GROUPS_PER_SLAB = TG // GMLP_GROUP_DIM
CAST_BLOCK_BYTES = 6 * 1024 * 1024
VMEM_LIMIT = 62 * 1024 * 1024


def _cparams(sem):
    return pltpu.CompilerParams(dimension_semantics=sem, vmem_limit_bytes=VMEM_LIMIT)


def _dot(a, b):
    return jnp.dot(a, b, preferred_element_type=F32)


def _rms(xf, g):
    ms = jnp.mean(xf * xf, axis=-1, keepdims=True)
    return xf * lax.rsqrt(ms + RMS_EPS) * g


def _rms_split(xf, g):
    ms = jnp.mean(xf * xf, axis=-1, keepdims=True)
    return (xf * g).astype(BF16), lax.rsqrt(ms + RMS_EPS)


def _gelu(z):
    return 0.5 * z * (1.0 + lax.erf(z * (1.0 / math.sqrt(2.0))))


def _cast_kernel(*refs):
    *in_refs, o_ref = refs
    col = 0
    for r in in_refs:
        w = r.shape[-1]
        o_ref[:, col:col + w] = r[...].astype(BF16)
        col += w


def _cast_bf16(layer, *ws):
    rows = ws[0].shape[1]
    cols = sum(w.shape[2] for w in ws)
    rb = rows
    while rb * cols * 4 > CAST_BLOCK_BYTES and rb % 32 == 0:
        rb //= 2
    return pl.pallas_call(
        _cast_kernel,
        grid=(rows // rb,),
        in_specs=[pl.BlockSpec((None, rb, w.shape[2]), lambda r: (layer, r, 0)) for w in ws],
        out_specs=pl.BlockSpec((rb, cols), lambda r: (r, 0)),
        out_shape=jax.ShapeDtypeStruct((rows, cols), BF16),
        compiler_params=_cparams(("parallel",)),
        name="cast",
    )(*ws)


class _CastJob(NamedTuple):
    layer: int
    arrays: tuple


def _cast_job_blocking(rows, n_tiles, n_steps):
    per_tile = rows // n_tiles
    for n_sub in range(n_steps, 0, -1):
        if per_tile % n_sub == 0 and (per_tile // n_sub) % BF16_SUBLANES == 0:
            return per_tile // n_sub, n_sub
    raise ValueError(f"cannot spread a {rows}-row cast over {n_tiles} tiles")


def _ffn_kernel(*refs, job_sizes):
    n_cast_in = sum(job_sizes)
    x_ref, g_ref, w1_ref, w3_ref, w2_ref = refs[:5]
    cast_in = refs[5:5 + n_cast_in]
    o_ref = refs[5 + n_cast_in]
    cast_out = refs[6 + n_cast_in:6 + n_cast_in + len(job_sizes)]
    h_ref, rs_ref = refs[-2:]
    f = pl.program_id(1)

    def swiglu(hg, rstd):
        a = _dot(hg, w1_ref[...]) * rstd
        b = _dot(hg, w3_ref[...]) * rstd
        gate = (0.5 * (a * jax.nn.sigmoid(a)) * b).astype(BF16)
        return _dot(gate, w2_ref[...])

    def side_casts():
        start = 0
        for n, out_ref in zip(job_sizes, cast_out):
            _cast_kernel(*cast_in[start:start + n], out_ref)
            start += n

    @pl.when(f == 0)
    def _():
        xf = x_ref[...]
        hg, rstd = _rms_split(xf, g_ref[...])
        h_ref[...] = hg
        rs_ref[...] = rstd
        o_ref[...] = xf + swiglu(hg, rstd)
        side_casts()

    @pl.when(f > 0)
    def _():
        o_ref[...] += swiglu(h_ref[...], rs_ref[...])
        side_casts()


def _ffn(x, g, w1, w3, w2, layer, jobs):
    t = x.shape[0]
    n_tiles, n_steps = t // TM_FFN, D_FF // TF
    in_specs = [
        pl.BlockSpec((TM_FFN, D_MODEL), lambda i, f: (i, 0)),
        pl.BlockSpec((None, 1, D_MODEL), lambda i, f: (layer, 0, 0)),
        pl.BlockSpec((D_MODEL, TF), lambda i, f: (0, f)),
        pl.BlockSpec((D_MODEL, TF), lambda i, f: (0, f)),
        pl.BlockSpec((TF, D_MODEL), lambda i, f: (f, 0)),
    ]
    out_specs = [pl.BlockSpec((TM_FFN, D_MODEL), lambda i, f: (i, 0))]
    out_shape = [jax.ShapeDtypeStruct((t, D_MODEL), F32)]
    cast_inputs = []
    for job in jobs:
        rows = job.arrays[0].shape[1]
        cols = sum(a.shape[2] for a in job.arrays)
        rb, n_sub = _cast_job_blocking(rows, n_tiles, n_steps)
        block = lambda i, f, n_sub=n_sub: i * n_sub + jnp.minimum(f, n_sub - 1)
        for a in job.arrays:
            in_specs.append(pl.BlockSpec(
                (None, rb, a.shape[2]),
                lambda i, f, block=block, l=job.layer: (l, block(i, f), 0)))
            cast_inputs.append(a)
        out_specs.append(pl.BlockSpec((rb, cols), lambda i, f, block=block: (block(i, f), 0)))
        out_shape.append(jax.ShapeDtypeStruct((rows, cols), BF16))
    y, *cast = pl.pallas_call(
        functools.partial(_ffn_kernel, job_sizes=tuple(len(job.arrays) for job in jobs)),
        grid=(n_tiles, n_steps),
        in_specs=in_specs,
        out_specs=out_specs,
        out_shape=out_shape,
        scratch_shapes=[pltpu.VMEM((TM_FFN, D_MODEL), BF16), pltpu.VMEM((TM_FFN, 1), F32)],
        compiler_params=_cparams(("arbitrary", "arbitrary")),
        name="ffn",
    )(x, g, w1, w3, w2, *cast_inputs)
    return y, cast


def _gmlp_kernel(x_ref, g_ref, win_ref, lng_ref, lnb_ref, ws_ref, bs_ref, wout_ref,
                 o_ref, h_ref, v_ref, s1_ref, s2_ref, mu_ref, rs_ref, gt0_ref, gt1_ref):
    j = pl.program_id(1)
    gt_refs = (gt0_ref, gt1_ref)

    def v_slab(h):
        z = _gelu(_dot(h, win_ref[...]))
        v_ref[j] = z
        p1 = z[:, :LANES]
        p2 = p1 * p1
        for c in range(1, TG // LANES):
            zc = z[:, c * LANES:(c + 1) * LANES]
            p1 += zc
            p2 += zc * zc
        return p1, p2

    @pl.when(j == 0)
    def _():
        xf = x_ref[...]
        o_ref[...] = xf
        h = _rms(xf, g_ref[...]).astype(BF16)
        h_ref[...] = h
        s1_ref[...], s2_ref[...] = v_slab(h)

    @pl.when((j > 0) & (j < N_HALF))
    def _():
        p1, p2 = v_slab(h_ref[...])
        s1_ref[...] += p1
        s2_ref[...] += p2

    def gate(k, dst_ref):
        u = _gelu(_dot(h_ref[...], win_ref[...]))
        if isinstance(k, int) and k == 0:
            mu = jnp.sum(s1_ref[...], axis=-1, keepdims=True) * (1.0 / GMLP_WIDTH)
            ex2 = jnp.sum(s2_ref[...], axis=-1, keepdims=True) * (1.0 / GMLP_WIDTH)
            rs = lax.rsqrt(ex2 - mu * mu + LN_EPS)
            mu_ref[...] = mu
            rs_ref[...] = rs
        else:
            mu, rs = mu_ref[...], rs_ref[...]
        vn = ((v_ref[k] - mu) * rs * lng_ref[k] + lnb_ref[k]).astype(BF16)
        row = lax.broadcasted_iota(jnp.int32, (CHUNK, CHUNK), 0)
        col = lax.broadcasted_iota(jnp.int32, (CHUNK, CHUNK), 1)
        causal = row >= col
        for gg in range(GROUPS_PER_SLAB):
            grp = GROUPS_PER_SLAB * k + gg
            w = jnp.where(causal, ws_ref[grp], 0.0).astype(BF16)
            bias = bs_ref[grp]
            cols = slice(gg * GMLP_GROUP_DIM, (gg + 1) * GMLP_GROUP_DIM)
            for c in range(TM // CHUNK):
                rows = slice(c * CHUNK, (c + 1) * CHUNK)
                s = _dot(w, vn[rows, cols]) + bias
                dst_ref[rows, cols] = (u[rows, cols] * s).astype(BF16)

    def project(src_ref):
        o_ref[...] += _dot(src_ref[...], wout_ref[...])

    @pl.when(j == N_HALF)
    def _():
        gate(0, gt_refs[0])

    for parity in range(2):
        @pl.when((j > N_HALF) & (j < 2 * N_HALF) & (((j - N_HALF) & 1) == parity))
        def _():
            project(gt_refs[1 - parity])
            gate(j - N_HALF, gt_refs[parity])

    @pl.when(j == 2 * N_HALF)
    def _():
        project(gt_refs[(N_HALF - 1) % 2])


def _gmlp(x, g, w_in, ln_g, ln_b, w_s, b_s, w_out, layer, mixer):
    t = x.shape[0]

    def win_map(i, j):
        return (0, jnp.where(j < N_HALF, j + N_HALF, jnp.minimum(j, 2 * N_HALF - 1) - N_HALF))

    def wout_map(i, j):
        return (jnp.where(j <= N_HALF, N_HALF - 1, j - N_HALF - 1), 0)

    def x_map(i, j):
        return (jnp.minimum(i + (j >= 2 * N_HALF - 1).astype(jnp.int32), t // TM - 1), 0)

    return pl.pallas_call(
        _gmlp_kernel,
        grid=(t // TM, 2 * N_HALF + 1),
        in_specs=[
            pl.BlockSpec((TM, D_MODEL), x_map),
            pl.BlockSpec((None, 1, D_MODEL), lambda i, j: (layer, 0, 0)),
            pl.BlockSpec((D_MODEL, TG), win_map),
            pl.BlockSpec((N_HALF, 1, TG), lambda i, j: (0, 0, 0)),
            pl.BlockSpec((N_HALF, 1, TG), lambda i, j: (0, 0, 0)),
            pl.BlockSpec((None, GMLP_GROUPS, CHUNK, CHUNK), lambda i, j: (mixer, 0, 0, 0)),
            pl.BlockSpec((GMLP_GROUPS, CHUNK, GMLP_GROUP_DIM), lambda i, j: (0, 0, 0)),
            pl.BlockSpec((TG, D_MODEL), wout_map),
        ],
        out_specs=pl.BlockSpec((TM, D_MODEL), lambda i, j: (i, 0)),
        out_shape=jax.ShapeDtypeStruct((t, D_MODEL), F32),
        scratch_shapes=[
            pltpu.VMEM((TM, D_MODEL), BF16),
            pltpu.VMEM((N_HALF, TM, TG), F32),
            pltpu.VMEM((TM, LANES), F32),
            pltpu.VMEM((TM, LANES), F32),
            pltpu.VMEM((TM, 1), F32),
            pltpu.VMEM((TM, 1), F32),
            pltpu.VMEM((TM, TG), BF16),
            pltpu.VMEM((TM, TG), BF16),
        ],
        compiler_params=_cparams(("parallel", "arbitrary")),
        name="gmlp",
    )(x, g, w_in, ln_g, ln_b, w_s, b_s, w_out)


def _rope(y, ca, cp, cm):
    n = y.shape[-1]
    reps = n // LANES
    ca = jnp.concatenate([ca] * reps, axis=-1)
    cp = jnp.concatenate([cp] * reps, axis=-1)
    cm = jnp.concatenate([cm] * reps, axis=-1)
    half = ROPE_DIM // 2
    return y * ca + pltpu.roll(y, n - half, 1) * cp + pltpu.roll(y, half, 1) * cm


def _dup_heads(y):
    lane = lax.broadcasted_iota(jnp.int32, (y.shape[0], LANES), 1)
    low = lane < HEAD_DIM
    out = []
    for p in range(y.shape[-1] // LANES):
        blk = y[:, p * LANES:(p + 1) * LANES]
        swapped = pltpu.roll(blk, HEAD_DIM, 1)
        out.append(jnp.where(low, blk, swapped))
        out.append(jnp.where(low, swapped, blk))
    return jnp.concatenate(out, axis=-1)


def _qkv_kernel(x_ref, g_ref, w_ref, b_ref, ca_ref, cp_ref, cm_ref, q_ref, k_ref, v_ref):
    hg, rstd = _rms_split(x_ref[...], g_ref[...])
    ca, cp, cm = ca_ref[...], cp_ref[...], cm_ref[...]
    for n in range(QW // TN):
        cols = slice(n * TN, (n + 1) * TN)
        y = _dot(hg, w_ref[:, cols]) * rstd + b_ref[:, cols]
        q_ref[:, cols] = (_rope(y, ca, cp, cm) * (HEAD_DIM ** -0.5 * LOG2_E)).astype(BF16)
    y = _dot(hg, w_ref[:, QW:]) * rstd + b_ref[:, QW:]
    k_ref[...] = _dup_heads(_rope(y[:, :KW], ca, cp, cm)).astype(BF16)
    v_ref[...] = _dup_heads(y[:, KW:]).astype(BF16)


def _qkv(x, g, w, b, ca, cp, cm, seq, layer, mixer):
    t = x.shape[0]
    wide = QW + 2 * KW
    pos_blocks = seq // TM
    tab = pl.BlockSpec((TM, LANES), lambda i: (i % pos_blocks, 0))
    kv_shape = jax.ShapeDtypeStruct((t, N_KV_HEADS * LANES), BF16)
    return pl.pallas_call(
        _qkv_kernel,
        grid=(t // TM,),
        in_specs=[
            pl.BlockSpec((TM, D_MODEL), lambda i: (i, 0)),
            pl.BlockSpec((None, 1, D_MODEL), lambda i: (layer, 0, 0)),
            pl.BlockSpec((D_MODEL, wide), lambda i: (0, 0)),
            pl.BlockSpec((None, 1, wide), lambda i: (mixer, 0, 0)),
            tab, tab, tab,
        ],
        out_specs=[
            pl.BlockSpec((TM, QW), lambda i: (i, 0)),
            pl.BlockSpec((TM, N_KV_HEADS * LANES), lambda i: (i, 0)),
            pl.BlockSpec((TM, N_KV_HEADS * LANES), lambda i: (i, 0)),
        ],
        out_shape=[jax.ShapeDtypeStruct((t, QW), BF16), kv_shape, kv_shape],
        compiler_params=_cparams(("parallel",)),
        name="qkv",
    )(x, g, w, b, ca, cp, cm)


def _attn_kernel(sink_ref, q_ref, kp_ref, kc_ref, vp_ref, vc_ref, x_ref, wo_ref, bo_ref,
                 o_ref, kbuf_ref, vbuf_ref, a_ref, *, tiles_per_seq):
    first_tile = (pl.program_id(0) % tiles_per_seq) == 0
    kbuf_ref[:WINDOW] = kp_ref[...]
    kbuf_ref[WINDOW:] = kc_ref[...]
    vbuf_ref[:WINDOW] = vp_ref[...]
    vbuf_ref[WINDOW:] = vc_ref[...]

    t_idx = lax.broadcasted_iota(jnp.int32, (WINDOW, 2 * WINDOW), 0)
    s_idx = lax.broadcasted_iota(jnp.int32, (WINDOW, 2 * WINDOW), 1)
    diff = t_idx + WINDOW - s_idx
    band = (diff >= 0) & (diff < WINDOW)
    lane = lax.broadcasted_iota(jnp.int32, (WINDOW, LANES), 1)
    low = lane < HEAD_DIM
    zero = jnp.zeros((), BF16)
    sink_col = lax.broadcasted_iota(jnp.int32, (1, 2 * WINDOW), 1) == 0
    sink_row = lax.broadcasted_iota(jnp.int32, (2 * WINDOW, LANES), 0) == 0

    def block(r, carry):
        row0 = pl.multiple_of(r * WINDOW, WINDOW)
        q_rows = pl.ds(row0, WINDOW)
        kv_rows = pl.ds(row0, 2 * WINDOW)
        has_prev = (r > 0) | jnp.logical_not(first_tile)
        valid = band & ((s_idx >= WINDOW) | has_prev)
        for kh in range(N_KV_HEADS):
            ks = slice(kh * LANES, (kh + 1) * LANES)
            k2 = kbuf_ref[kv_rows, ks]
            v2 = jnp.where(sink_row, zero, vbuf_ref[kv_rows, ks])
            q_stack = []
            for hq in range(Q_PER_KV):
                p = kh * (Q_PER_KV // 2) + hq // 2
                qb = q_ref[q_rows, p * LANES:(p + 1) * LANES]
                q_stack.append(jnp.where(low if hq % 2 == 0 else ~low, qb, zero))
            s_all = lax.dot_general(jnp.concatenate(q_stack, axis=0), k2, (((1,), (1,)), ((), ())),
                                    preferred_element_type=F32)
            e_stack, recips = [], []
            for hq in range(Q_PER_KV):
                head = kh * Q_PER_KV + hq
                fill = jnp.where(sink_col, sink_ref[head] * LOG2_E, -jnp.inf)
                s = jnp.where(valid, s_all[hq * WINDOW:(hq + 1) * WINDOW], fill)
                e = jnp.exp2(s - jnp.max(s, axis=-1, keepdims=True))
                recips.append(1.0 / jnp.sum(e, axis=-1, keepdims=True))
                e_stack.append(e.astype(BF16))
            o_all = _dot(jnp.concatenate(e_stack, axis=0), v2)
            for pp in range(Q_PER_KV // 2):
                p = kh * (Q_PER_KV // 2) + pp
                halves = [o_all[hq * WINDOW:(hq + 1) * WINDOW] * recips[hq] for hq in (2 * pp, 2 * pp + 1)]
                a_ref[q_rows, p * LANES:(p + 1) * LANES] = (
                    jnp.where(low, halves[0], halves[1]).astype(BF16))
        return carry

    lax.fori_loop(0, TM // WINDOW, block, 0)
    o_ref[...] = x_ref[...] + _dot(a_ref[...], wo_ref[...]) + bo_ref[...]


def _attn(sinks, q, k2, v2, x, wo, bo, seq, mixer):
    t = q.shape[0]
    kvw = N_KV_HEADS * LANES
    blocks_per_tile = TM // WINDOW
    cur = lambda i: (i, 0)
    prev = lambda i: (jnp.maximum(i * blocks_per_tile - 1, 0), 0)
    return pl.pallas_call(
        functools.partial(_attn_kernel, tiles_per_seq=seq // TM),
        grid=(t // TM,),
        in_specs=[
            pl.BlockSpec(memory_space=pltpu.SMEM),
            pl.BlockSpec((TM, QW), cur),
            pl.BlockSpec((WINDOW, kvw), prev),
            pl.BlockSpec((TM, kvw), cur),
            pl.BlockSpec((WINDOW, kvw), prev),
            pl.BlockSpec((TM, kvw), cur),
            pl.BlockSpec((TM, D_MODEL), cur),
            pl.BlockSpec((QW, D_MODEL), lambda i: (0, 0)),
            pl.BlockSpec((None, 1, D_MODEL), lambda i: (mixer, 0, 0)),
        ],
        out_specs=pl.BlockSpec((TM, D_MODEL), cur),
        out_shape=jax.ShapeDtypeStruct((t, D_MODEL), F32),
        scratch_shapes=[pltpu.VMEM((WINDOW + TM, kvw), BF16), pltpu.VMEM((WINDOW + TM, kvw), BF16),
                        pltpu.VMEM((TM, QW), BF16)],
        compiler_params=_cparams(("parallel",)),
        name="attn",
    )(sinks, q, k2, k2, v2, v2, x, wo, bo)


def _ple_kernel(x_ref, p_ref, g_ref, wg_ref, wp_ref, fg_ref, o_ref, *, final):
    xf = x_ref[...]
    hg, rstd = _rms_split(xf, g_ref[...])
    gate = jax.nn.sigmoid(_dot(hg, wg_ref[...]) * rstd)
    proj = _dot(p_ref[...].astype(BF16), wp_ref[...])
    y = xf + gate * proj
    if final:
        y = _rms(y, fg_ref[...])
    o_ref[...] = y


def _ple(x, p, g, wg, wp, fg, layer, final):
    t = x.shape[0]
    return pl.pallas_call(
        functools.partial(_ple_kernel, final=final),
        grid=(t // TM,),
        in_specs=[
            pl.BlockSpec((TM, D_MODEL), lambda i: (i, 0)),
            pl.BlockSpec((None, TM, PLE_DIM), lambda i: (layer, i, 0)),
            pl.BlockSpec((None, 1, D_MODEL), lambda i: (layer, 0, 0)),
            pl.BlockSpec((D_MODEL, D_MODEL), lambda i: (0, 0)),
            pl.BlockSpec((PLE_DIM, D_MODEL), lambda i: (0, 0)),
            pl.BlockSpec((1, D_MODEL), lambda i: (0, 0)),
        ],
        out_specs=pl.BlockSpec((TM, D_MODEL), lambda i: (i, 0)),
        out_shape=jax.ShapeDtypeStruct((t, D_MODEL), F32),
        compiler_params=_cparams(("parallel",)),
        name="ple_final" if final else "ple",
    )(x, p, g, wg, wp, fg)


def _rope_coefficients(seq):
    half = ROPE_DIM // 2
    inv_freq = np.float32(ROPE_THETA) ** (-np.arange(0, ROPE_DIM, 2, dtype=np.float32) / np.float32(ROPE_DIM))
    ang = (np.arange(seq, dtype=np.float32)[:, None] * inv_freq[None, :]).astype(np.float32)
    cos = np.cos(ang.astype(np.float64)).astype(np.float32)
    sin = np.sin(ang.astype(np.float64)).astype(np.float32)
    ones = np.ones((seq, HEAD_DIM - ROPE_DIM), np.float32)
    zeros = np.zeros((seq, HEAD_DIM - half), np.float32)
    ca = np.concatenate([cos, cos, ones], axis=-1)
    cp = np.concatenate([-sin, zeros], axis=-1)
    cm = np.concatenate([np.zeros((seq, half), np.float32), sin, zeros[:, half:]], axis=-1)
    tile = lambda a: jnp.asarray(np.concatenate([a] * (LANES // HEAD_DIM), axis=-1))
    return tile(ca), tile(cp), tile(cm)


def kernel(x, p, ffn1_norm, ffn1_w1, ffn1_w3, ffn1_w2, mix_norm, ffn2_norm, ffn2_w1, ffn2_w3, ffn2_w2, ple_norm, ple_w_gate, ple_w_proj, gmlp_w_in, gmlp_ln_g, gmlp_ln_b, gmlp_w_s, gmlp_b_s, gmlp_w_out, swa_wq, swa_bq, swa_wk, swa_bk, swa_wv, swa_bv, swa_sinks, swa_wo, swa_bo, final_norm):
    batch, seq, _ = x.shape
    t = batch * seq
    assert seq % TM == 0 and TM % WINDOW == 0 and t % TM_FFN == 0
    rows = lambda a: a.reshape(a.shape[0], 1, a.shape[1])
    xs = x.reshape(t, D_MODEL)
    ps = p.reshape(DEPTH, t, PLE_DIM)
    ca, cp, cm = _rope_coefficients(seq)

    ffn1_g, ffn2_g, mix_g, ple_g = rows(ffn1_norm), rows(ffn2_norm), rows(mix_norm), rows(ple_norm)
    swa_bqkv = rows(jnp.concatenate([swa_bq, swa_bk, swa_bv], axis=1))
    fg = final_norm.reshape(1, D_MODEL)

    ffn_w = [_cast_bf16(0, w) for w in (ffn1_w1, ffn1_w3, ffn1_w2)]
    for i in range(DEPTH):
        j = i // 2
        if i % 2 == 0:
            mixer_jobs = [_CastJob(j, (gmlp_w_in,)), _CastJob(j, (gmlp_w_out,))]
        else:
            mixer_jobs = [_CastJob(j, (swa_wq, swa_wk, swa_wv)), _CastJob(j, (swa_wo,))]
        ffn2_jobs = [_CastJob(i, (w,)) for w in (ffn2_w1, ffn2_w3, ffn2_w2)]
        xs, (mix_w0, mix_w1, *ffn_w) = _ffn(xs, ffn1_g, *ffn_w, layer=i, jobs=mixer_jobs + ffn2_jobs)
        if i % 2 == 0:
            b_s = jnp.broadcast_to(gmlp_b_s[j][:, :, None], (GMLP_GROUPS, CHUNK, GMLP_GROUP_DIM))
            xs = _gmlp(xs, mix_g, mix_w0,
                       gmlp_ln_g[j].reshape(N_HALF, 1, TG), gmlp_ln_b[j].reshape(N_HALF, 1, TG),
                       gmlp_w_s, b_s, mix_w1, layer=i, mixer=j)
        else:
            q, k2, v2 = _qkv(xs, mix_g, mix_w0, swa_bqkv, ca, cp, cm, seq, layer=i, mixer=j)
            xs = _attn(swa_sinks[j], q, k2, v2, xs, mix_w1, rows(swa_bo), seq, mixer=j)
        ple_jobs = [_CastJob(i, (ple_w_gate,)), _CastJob(i, (ple_w_proj,))]
        next_jobs = [_CastJob(i + 1, (w,)) for w in (ffn1_w1, ffn1_w3, ffn1_w2)] if i + 1 < DEPTH else []
        xs, (ple_wg, ple_wp, *ffn_w) = _ffn(xs, ffn2_g, *ffn_w, layer=i, jobs=ple_jobs + next_jobs)
        xs = _ple(xs, ps, ple_g, ple_wg, ple_wp, fg, layer=i, final=(i == DEPTH - 1))
    return xs.reshape(batch, seq, D_MODEL)
```
